```python
import math
import jax, jax.numpy as jnp
from jax import lax
import numpy as np

D_MODEL = 1024
BATCH = 2
SEQ = 8192
DEPTH = 4
DEC_BATCH = 32
DEC_SEQ = 8
PAST_LEN = 8192
PAGE_SIZE = 128

HEAD_DIM = 64
H_A = 8
KV_A = 4
G_A = H_A // KV_A
MOBA_BLOCK = 256
MOBA_TOPK = 3
MOBA_QB = 64
H_B = 8
KV_B = 4
G_B = H_B // KV_B
FORGET_BIAS = 2.0
H_C = 4
KV_C = 2
G_C = H_C // KV_C
V_DIM_C = 2 * HEAD_DIM
W_A = H_A * HEAD_DIM
W_B = H_B * HEAD_DIM
W_C = H_C * V_DIM_C
N_BRANCH = 3
Q_BLOCK = 128
RMS_EPS = 1e-6
IN_SIZES = (
    H_A * HEAD_DIM, KV_A * HEAD_DIM, KV_A * HEAD_DIM, W_A,
    H_B * HEAD_DIM, KV_B * HEAD_DIM, KV_B * HEAD_DIM, H_B, W_B,
    2 * H_C * HEAD_DIM, 2 * KV_C * HEAD_DIM, KV_C * V_DIM_C, W_C,
    N_BRANCH * D_MODEL,
)
D_IN = sum(IN_SIZES)

kernel_name = "hybrid_moba_fox_diff_gated_decoder_step"


def _rmsnorm(x, g):
    x32 = x.astype(jnp.float32)
    y = x32 * lax.rsqrt(jnp.mean(x32 * x32, axis=-1, keepdims=True) + RMS_EPS)
    return (y * g.astype(jnp.float32)).astype(x.dtype)


def _alibi_slopes(n):
    return 2.0 ** (-8.0 * jnp.arange(1, n + 1, dtype=jnp.float32) / n)


def _gather_pages(pool, page_table):
    g = pool[page_table]
    return g.reshape((g.shape[0], g.shape[1] * g.shape[2]) + g.shape[3:])


def _dense_attention(q, k, v, coef, qe, ke, q_pos, k_pos):
    B, Tq = q.shape[0], q.shape[1]
    qb = Q_BLOCK if Tq % Q_BLOCK == 0 else Tq
    nb = Tq // qb
    scale = HEAD_DIM ** -0.5

    def blocks(a):
        return jnp.moveaxis(a.reshape((B, nb, qb) + a.shape[2:]), 1, 0)

    ke_t = jnp.moveaxis(ke.astype(jnp.float32), 1, -1)[:, :, :, None, :]
    coef32 = coef.astype(jnp.float32)

    def one(args):
        qblk, qeblk, qpos = args
        s = jnp.einsum('bqmhgd,bkmhd->bmhgqk', qblk, k).astype(jnp.float32) * scale
        bias = jnp.moveaxis(qeblk.astype(jnp.float32), 1, -1)[..., None] - ke_t
        s = s + bias[:, None]
        s = jnp.where(k_pos[None, :] <= qpos[:, None], s, -jnp.inf)
        p = jax.nn.softmax(s, axis=-1)
        p = jnp.einsum('m,bmhgqk->bhgqk', coef32, p)
        return jnp.einsum('bhgqk,bkhd->bqhgd', p.astype(v.dtype), v)

    out = lax.map(one, (blocks(q), blocks(qe), q_pos.reshape(nb, qb)))
    return jnp.moveaxis(out, 0, 1).reshape((B, Tq) + out.shape[3:])


def _moba_attention(q, k, v, q_pos, slopes):
    B, Tq, Hk, G, d = q.shape
    Tk = k.shape[1]
    nblk = -(-Tk // MOBA_BLOCK)
    pad = nblk * MOBA_BLOCK - Tk
    kb = jnp.pad(k, ((0, 0), (0, pad), (0, 0), (0, 0))).reshape(B, nblk, MOBA_BLOCK, Hk, d)
    vb = jnp.pad(v, ((0, 0), (0, pad), (0, 0), (0, 0))).reshape(B, nblk, MOBA_BLOCK, Hk, d)
    k_mean = jnp.mean(kb.astype(jnp.float32), axis=2)
    kbt = jnp.moveaxis(kb, 3, 1)
    vbt = jnp.moveaxis(vb, 3, 1)
    qb = MOBA_QB if Tq % MOBA_QB == 0 else Tq
    nq = Tq // qb
    topk = min(MOBA_TOPK, nblk)
    b_ix = jnp.arange(B)[:, None, None, None, None]
    h_ix = jnp.arange(Hk)[None, None, :, None, None]
    blk_pos = jnp.arange(MOBA_BLOCK, dtype=jnp.int32)
    sl = slopes.reshape(Hk, G)[None, None, :, :, None, None]
    scale = HEAD_DIM ** -0.5

    def one(args):
        qblk, qpos = args
        own = qpos // MOBA_BLOCK
        gate = jnp.einsum('bqhgd,bnhd->bqhgn', qblk.astype(jnp.float32), k_mean)
        past = jnp.arange(nblk)[None, :] < own[:, None]
        gate = jnp.where(past[None, :, None, None, :], gate, -jnp.inf)
        _, sel = lax.top_k(gate, topk)
        sel_ok = jnp.arange(topk)[None, :] < own[:, None]
        own_b = jnp.broadcast_to(own[None, :, None, None, None], sel.shape[:-1] + (1,)).astype(sel.dtype)
        idx = jnp.concatenate([sel, own_b], axis=-1)
        kg = kbt[b_ix, h_ix, idx]
        vg = vbt[b_ix, h_ix, idx]
        kpos = idx[..., None] * MOBA_BLOCK + blk_pos
        dist = qpos[None, :, None, None, None, None] - kpos
        s = jnp.einsum('bqhgd,bqhgjkd->bqhgjk', qblk, kg).astype(jnp.float32) * scale
        s = s - sl * dist.astype(jnp.float32)
        ok_blk = jnp.concatenate([jnp.broadcast_to(sel_ok[None, :, None, None, :], sel.shape),
                                  jnp.ones(sel.shape[:-1] + (1,), dtype=bool)], axis=-1)
        ok = ok_blk[..., None] & (dist >= 0)
        s = jnp.where(ok, s, -jnp.inf)
        p = jax.nn.softmax(s.reshape(s.shape[:4] + (-1,)), axis=-1).reshape(s.shape)
        return jnp.einsum('bqhgjk,bqhgjkd->bqhgd', p.astype(vg.dtype), vg)

    qs = jnp.moveaxis(q.reshape(B, nq, qb, Hk, G, d), 1, 0)
    out = lax.map(one, (qs, q_pos.reshape(nq, qb)))
    return jnp.moveaxis(out, 0, 1).reshape(B, Tq, Hk, G, d)


def _layer(x, past, start, layer_idx, w_in, w_o_a, w_o_b, w_o_c, w_out, norm_g, forget_b, diff_lambda, diff_subln_g):
    B, T, _ = x.shape
    h = _rmsnorm(x, norm_g)
    z = h @ w_in
    (qa, ka, va, ga, qb_, kb_, vb_, fb, gb, qc, kc, vc, gc, gm) = jnp.split(
        z, np.cumsum(IN_SIZES)[:-1].tolist(), axis=-1)

    new_a = jnp.concatenate([ka.reshape(B, T, KV_A, HEAD_DIM), va.reshape(B, T, KV_A, HEAD_DIM)], axis=-1)
    new_b = jnp.concatenate([kb_.reshape(B, T, KV_B, HEAD_DIM), vb_.reshape(B, T, KV_B, HEAD_DIM)], axis=-1)
    new_logf = jax.nn.log_sigmoid(fb + forget_b)
    kc_rows = jnp.moveaxis(kc.reshape(B, T, 2, KV_C, HEAD_DIM), 2, 3).reshape(B, T, KV_C, 2 * HEAD_DIM)
    new_c = jnp.concatenate([kc_rows, vc.reshape(B, T, KV_C, V_DIM_C)], axis=-1)

    if past is None:
        full_a, full_b, full_logf, full_c = new_a, new_b, new_logf, new_c
    else:
        full_a = jnp.concatenate([past[0], new_a], axis=1)
        full_b = jnp.concatenate([past[1], new_b], axis=1)
        full_logf = jnp.concatenate([past[2], new_logf], axis=1)
        full_c = jnp.concatenate([past[3], new_c], axis=1)
    Tk = start + T
    q_pos = start + jnp.arange(T, dtype=jnp.int32)
    k_pos = jnp.arange(Tk, dtype=jnp.int32)

    k_a, v_a = jnp.split(full_a, 2, axis=-1)
    o_a = _moba_attention(qa.reshape(B, T, KV_A, G_A, HEAD_DIM), k_a, v_a, q_pos, _alibi_slopes(H_A))
    o_a = o_a.reshape(B, T, W_A) * jax.nn.silu(ga)

    k_b, v_b = jnp.split(full_b, 2, axis=-1)
    c_cum = jnp.cumsum(full_logf.astype(jnp.float32), axis=1).reshape(B, Tk, KV_B, G_B)
    o_b = _dense_attention(qb_.reshape(B, T, 1, KV_B, G_B, HEAD_DIM), k_b[:, :, None], v_b,
                           jnp.ones((1,), jnp.float32), c_cum[:, start:], c_cum, q_pos, k_pos)
    o_b = o_b.reshape(B, T, W_B) * jax.nn.silu(gb)

    k_c = jnp.moveaxis(full_c[..., :2 * HEAD_DIM].reshape(B, Tk, KV_C, 2, HEAD_DIM), 3, 2)
    v_c = full_c[..., 2 * HEAD_DIM:]
    lam = diff_lambda.astype(jnp.float32)
    lam_init = 0.8 - 0.6 * math.exp(-0.3 * layer_idx)
    lam_val = jnp.exp(jnp.sum(lam[0] * lam[1])) - jnp.exp(jnp.sum(lam[2] * lam[3])) + lam_init
    coef = jnp.stack([jnp.ones((), jnp.float32), -lam_val])
    sl_c = _alibi_slopes(H_C).reshape(KV_C, G_C)
    qe_c = jnp.broadcast_to(-sl_c * q_pos.astype(jnp.float32)[:, None, None], (B, T, KV_C, G_C))
    ke_c = jnp.broadcast_to(-sl_c * k_pos.astype(jnp.float32)[:, None, None], (B, Tk, KV_C, G_C))
    o_c = _dense_attention(qc.reshape(B, T, 2, KV_C, G_C, HEAD_DIM), k_c, v_c, coef, qe_c, ke_c, q_pos, k_pos)
    o_c = _rmsnorm(o_c, diff_subln_g) * (1.0 - lam_init)
    o_c = o_c.reshape(B, T, W_C) * jax.nn.silu(gc)

    gates = jax.nn.sigmoid(gm.reshape(B, T, N_BRANCH, D_MODEL))
    merged = (gates[:, :, 0] * (o_a @ w_o_a) + gates[:, :, 1] * (o_b @ w_o_b)
              + gates[:, :, 2] * (o_c @ w_o_c))
    return x + merged @ w_out, new_a, new_b, new_logf, new_c


def setup_inputs(seed: int = 0) -> dict:
    key = jax.random.key(seed)
    ks = jax.random.split(key, 20)
    n_pages = PAST_LEN // PAGE_SIZE
    n_used = DEC_BATCH * n_pages
    n_pool = n_used + n_used // 4
    nrm = jax.random.normal
    x_prompt = nrm(ks[0], (BATCH, SEQ, D_MODEL), jnp.float32)
    x_sample = nrm(ks[1], (DEC_BATCH, DEC_SEQ, D_MODEL), jnp.float32)
    cache_a_kv = nrm(ks[2], (DEPTH, n_pool, PAGE_SIZE, KV_A, 2 * HEAD_DIM), jnp.float32)
    cache_b_kv = nrm(ks[3], (DEPTH, n_pool, PAGE_SIZE, KV_B, 2 * HEAD_DIM), jnp.float32)
    cache_b_logf = jax.nn.log_sigmoid(FORGET_BIAS + 0.5 * nrm(ks[4], (DEPTH, n_pool, PAGE_SIZE, H_B), jnp.float32))
    cache_c_kv = nrm(ks[5], (DEPTH, n_pool, PAGE_SIZE, KV_C, 2 * HEAD_DIM + V_DIM_C), jnp.float32)
    page_table = jax.random.permutation(ks[6], n_pool)[:n_used].reshape(DEC_BATCH, n_pages).astype(jnp.int32)
    w_in = nrm(ks[7], (DEPTH, D_MODEL, D_IN), jnp.float32) * D_MODEL ** -0.5
    w_o_a = nrm(ks[8], (DEPTH, W_A, D_MODEL), jnp.float32) * W_A ** -0.5
    w_o_b = nrm(ks[9], (DEPTH, W_B, D_MODEL), jnp.float32) * W_B ** -0.5
    w_o_c = nrm(ks[10], (DEPTH, W_C, D_MODEL), jnp.float32) * W_C ** -0.5
    w_out = nrm(ks[11], (DEPTH, D_MODEL, D_MODEL), jnp.float32) * D_MODEL ** -0.5
    norm_g = 1.0 + 0.1 * nrm(ks[12], (DEPTH, D_MODEL), jnp.float32)
    forget_b = FORGET_BIAS + 0.3 * nrm(ks[13], (DEPTH, H_B), jnp.float32)
    diff_lambda = 0.1 * nrm(ks[14], (DEPTH, 4, HEAD_DIM), jnp.float32)
    diff_subln_g = 1.0 + 0.1 * nrm(ks[15], (DEPTH, V_DIM_C), jnp.float32)
    final_norm_g = 1.0 + 0.1 * nrm(ks[16], (D_MODEL,), jnp.float32)
    return {"x_prompt": x_prompt, "x_sample": x_sample,
            "cache_a_kv": cache_a_kv, "cache_b_kv": cache_b_kv, "cache_b_logf": cache_b_logf,
            "cache_c_kv": cache_c_kv, "page_table": page_table,
            "w_in": w_in, "w_o_a": w_o_a, "w_o_b": w_o_b, "w_o_c": w_o_c, "w_out": w_out,
            "norm_g": norm_g, "forget_b": forget_b, "diff_lambda": diff_lambda,
            "diff_subln_g": diff_subln_g, "final_norm_g": final_norm_g}


def reference(x_prompt, x_sample, cache_a_kv, cache_b_kv, cache_b_logf, cache_c_kv, page_table,
              w_in, w_o_a, w_o_b, w_o_c, w_out, norm_g, forget_b, diff_lambda, diff_subln_g, final_norm_g):
    xp, xs = x_prompt, x_sample
    pa, pb, pf, pc = [], [], [], []
    sa, sb, sf, sc = [], [], [], []
    for l in range(DEPTH):
        params = (w_in[l], w_o_a[l], w_o_b[l], w_o_c[l], w_out[l], norm_g[l], forget_b[l],
                  diff_lambda[l], diff_subln_g[l])
        xp, ra, rb, rf, rc = _layer(xp, None, 0, l, *params)
        pa.append(ra); pb.append(rb); pf.append(rf); pc.append(rc)
        past = (_gather_pages(cache_a_kv[l], page_table), _gather_pages(cache_b_kv[l], page_table),
                _gather_pages(cache_b_logf[l], page_table), _gather_pages(cache_c_kv[l], page_table))
        xs, ra, rb, rf, rc = _layer(xs, past, PAST_LEN, l, *params)
        sa.append(ra); sb.append(rb); sf.append(rf); sc.append(rc)
    y_prompt = _rmsnorm(xp, final_norm_g)
    y_sample = _rmsnorm(xs, final_norm_g)
    new_a_prompt = jnp.stack(pa)
    new_b_prompt = jnp.stack(pb)
    new_logf_prompt = jnp.stack(pf)
    new_c_prompt = jnp.stack(pc)
    new_a_sample = jnp.stack(sa)
    new_b_sample = jnp.stack(sb)
    new_logf_sample = jnp.stack(sf)
    new_c_sample = jnp.stack(sc)
    return (y_prompt, y_sample, new_a_prompt, new_b_prompt, new_logf_prompt, new_c_prompt,
            new_a_sample, new_b_sample, new_logf_sample, new_c_sample)
```

```python
import functools
import math

import jax
import jax.numpy as jnp
import numpy as np
from jax import lax
from jax.experimental import pallas as pl
from jax.experimental.pallas import tpu as pltpu

F32 = jnp.float32
BF16 = jnp.bfloat16

D_MODEL = 1024
HEAD_DIM = 64
PAGE_SIZE = 128
H_A, KV_A = 8, 4
H_B, KV_B = 8, 4
H_C, KV_C = 4, 2
GROUP = 2
V_DIM_C = 2 * HEAD_DIM
MOBA_BLOCK = 256
MOBA_TOPK = 3
N_BRANCH = 3
RMS_EPS = 1e-6
W_BRANCH = 512
IN_NAMES = ("qa", "ka", "va", "ga", "qb", "kb", "vb", "fb", "gb", "qc", "kc", "vc", "gc", "gm")
IN_SIZES = (512, 256, 256, 512, 512, 256, 256, 8, 512, 512, 256, 256, 512, N_BRANCH * D_MODEL)
D_IN = sum(IN_SIZES)

NEG = -1e30
LANES = 128
FB_PAD = LANES
OFF_KVA, OFF_KVB, OFF_KVC, OFF_FB = 0, 512, 1024, 1536
OFF_Q = OFF_FB + FB_PAD
OFF_G = OFF_Q + 3 * W_BRANCH
OFF_GM = OFF_G + 3 * W_BRANCH
D_PERM = OFF_GM + N_BRANCH * D_MODEL
VMEM_LIMIT = 56 * 1024 * 1024


def _perm_and_scale():
    off = dict(zip(IN_NAMES, np.concatenate([[0], np.cumsum(IN_SIZES)[:-1]])))
    d = HEAD_DIM
    cols = []
    for kn, vn, nkv in (("ka", "va", KV_A), ("kb", "vb", KV_B)):
        for h in range(nkv):
            cols += list(range(off[kn] + h * d, off[kn] + (h + 1) * d))
            cols += list(range(off[vn] + h * d, off[vn] + (h + 1) * d))
    for h in range(KV_C):
        for m in range(2):
            cols += list(range(off["kc"] + (m * KV_C + h) * d, off["kc"] + (m * KV_C + h + 1) * d))
        cols += list(range(off["vc"] + h * V_DIM_C, off["vc"] + (h + 1) * V_DIM_C))
    cols += list(range(off["fb"], off["fb"] + H_B)) + [D_IN] * (FB_PAD - H_B)
    cols += list(range(off["qa"], off["qa"] + 512))
    cols += list(range(off["qb"], off["qb"] + 512))
    for h in range(KV_C):
        for m in range(2):
            for g in range(GROUP):
                s = off["qc"] + ((m * KV_C + h) * GROUP + g) * d
                cols += list(range(s, s + d))
    for n in ("ga", "gb", "gc"):
        cols += list(range(off[n], off[n] + 512))
    cols += list(range(off["gm"], off["gm"] + N_BRANCH * D_MODEL))
    perm = np.asarray(cols, np.int32)
    assert perm.shape[0] == D_PERM
    scale = np.ones((D_PERM,), np.float32)
    scale[OFF_Q:OFF_G] = HEAD_DIM ** -0.5
    return perm, scale


_PERM, _SCALE = _perm_and_scale()


def _params(n_axes):
    return pltpu.CompilerParams(dimension_semantics=("arbitrary",) * n_axes, vmem_limit_bytes=VMEM_LIMIT)


def _sigmoid(x):
    return 1.0 / (1.0 + jnp.exp(-x))


def _dot_t(a, b, precision=None):
    return lax.dot_general(a, b, (((1,), (1,)), ((), ())), preferred_element_type=F32, precision=precision)


def _inproj_kernel(x_ref, g_ref, fb_ref, w_ref, kva_ref, kvb_ref, kvc_ref, logf_ref, kvbf_ref, q_ref, gate_ref,
                   gm_ref):
    x = x_ref[...]
    ms = jnp.mean(x * x, axis=-1, keepdims=True)
    h = ((x * lax.rsqrt(ms + RMS_EPS)) * g_ref[...]).astype(BF16)

    def mm(a, n):
        return jnp.dot(h, w_ref[:, a:a + n], preferred_element_type=F32)

    for i, ref in enumerate((kva_ref, kvb_ref, kvc_ref)):
        z = mm(i * W_BRANCH, W_BRANCH)
        ref[...] = z
        kvbf_ref[:, i * W_BRANCH:(i + 1) * W_BRANCH] = z.astype(BF16)
    zf = mm(OFF_FB, FB_PAD) + fb_ref[...]
    lf = jnp.minimum(zf, 0.0) - jnp.log(1.0 + jnp.exp(-jnp.abs(zf)))
    logf_ref[...] = lf[:, :H_B]
    for i in range(3):
        q_ref[:, i * W_BRANCH:(i + 1) * W_BRANCH] = mm(OFF_Q + i * W_BRANCH, W_BRANCH).astype(q_ref.dtype)
        gate_ref[:, i * W_BRANCH:(i + 1) * W_BRANCH] = mm(OFF_G + i * W_BRANCH, W_BRANCH).astype(BF16)
    for i in range(N_BRANCH * D_MODEL // W_BRANCH):
        gm_ref[:, i * W_BRANCH:(i + 1) * W_BRANCH] = mm(OFF_GM + i * W_BRANCH, W_BRANCH).astype(BF16)


def _inproj(x2d, layer, norm_g3, fb3, w_perm, *, tm, q_dtype):
    n = x2d.shape[0]
    row = lambda w: pl.BlockSpec((tm, w), lambda i: (i, 0))
    lay = lambda shp: pl.BlockSpec((None,) + shp, lambda i: (layer, 0, 0))
    out_shape = (
        jax.ShapeDtypeStruct((n, W_BRANCH), F32), jax.ShapeDtypeStruct((n, W_BRANCH), F32),
        jax.ShapeDtypeStruct((n, W_BRANCH), F32), jax.ShapeDtypeStruct((n, H_B), F32),
        jax.ShapeDtypeStruct((n, 3 * W_BRANCH), BF16), jax.ShapeDtypeStruct((n, 3 * W_BRANCH), q_dtype),
        jax.ShapeDtypeStruct((n, 3 * W_BRANCH), BF16), jax.ShapeDtypeStruct((n, N_BRANCH * D_MODEL), BF16))
    return pl.pallas_call(
        _inproj_kernel, grid=(n // tm,),
        in_specs=[row(D_MODEL), lay((1, D_MODEL)), lay((1, FB_PAD)), lay((D_MODEL, D_PERM))],
        out_specs=[row(W_BRANCH), row(W_BRANCH), row(W_BRANCH), row(H_B), row(3 * W_BRANCH), row(3 * W_BRANCH),
                   row(3 * W_BRANCH), row(N_BRANCH * D_MODEL)],
        out_shape=out_shape, compiler_params=_params(1), name="inproj")(x2d, norm_g3, fb3, w_perm)


def _outproj_kernel(x_ref, oa_ref, ob_ref, oc_ref, gate_ref, gm_ref, woa_ref, wob_ref, woc_ref, wout_ref, fg_ref,
                    y_ref, *, final):
    def branch(i, o_ref, w_ref):
        g = gate_ref[:, i * W_BRANCH:(i + 1) * W_BRANCH].astype(F32)
        u = (o_ref[...].astype(F32) * (g * _sigmoid(g))).astype(BF16)
        p = jnp.dot(u, w_ref[...], preferred_element_type=F32)
        return _sigmoid(gm_ref[:, i * D_MODEL:(i + 1) * D_MODEL].astype(F32)) * p

    merged = branch(0, oa_ref, woa_ref) + branch(1, ob_ref, wob_ref) + branch(2, oc_ref, woc_ref)
    y = x_ref[...] + jnp.dot(merged.astype(BF16), wout_ref[...], preferred_element_type=F32)
    if final:
        ms = jnp.mean(y * y, axis=-1, keepdims=True)
        y = (y * lax.rsqrt(ms + RMS_EPS)) * fg_ref[...]
    y_ref[...] = y


def _outproj(x2d, oa, ob, oc, gates, gm, layer, woa, wob, woc, wout, fg2, *, tm, final):
    n = x2d.shape[0]
    row = lambda w: pl.BlockSpec((tm, w), lambda i: (i, 0))
    lay = lambda shp: pl.BlockSpec((None,) + shp, lambda i: (layer, 0, 0))
    return pl.pallas_call(
        functools.partial(_outproj_kernel, final=final), grid=(n // tm,),
        in_specs=[row(D_MODEL), row(W_BRANCH), row(W_BRANCH), row(W_BRANCH), row(3 * W_BRANCH),
                  row(N_BRANCH * D_MODEL), lay((W_BRANCH, D_MODEL)), lay((W_BRANCH, D_MODEL)),
                  lay((W_BRANCH, D_MODEL)), lay((D_MODEL, D_MODEL)), pl.BlockSpec((1, D_MODEL), lambda i: (0, 0))],
        out_specs=row(D_MODEL), out_shape=jax.ShapeDtypeStruct((n, D_MODEL), F32),
        compiler_params=_params(1), name="outproj")(x2d, oa, ob, oc, gates, gm, woa, wob, woc, wout, fg2)


def _kmean_kernel(kv_ref, o_ref, *, nb):
    for i in range(nb):
        o_ref[i:i + 1, :] = jnp.mean(kv_ref[i * MOBA_BLOCK:(i + 1) * MOBA_BLOCK, :], axis=0, keepdims=True)


def _kmean(kva2d):
    nblk = kva2d.shape[0] // MOBA_BLOCK
    nb = 8 if nblk % 8 == 0 else nblk
    return pl.pallas_call(
        functools.partial(_kmean_kernel, nb=nb), grid=(nblk // nb,),
        in_specs=[pl.BlockSpec((nb * MOBA_BLOCK, W_BRANCH), lambda i: (i, 0))],
        out_specs=pl.BlockSpec((nb, W_BRANCH), lambda i: (i, 0)),
        out_shape=jax.ShapeDtypeStruct((nblk, W_BRANCH), F32), compiler_params=_params(1), name="kmean")(kva2d)


def _upper_ones(n):
    r = lax.broadcasted_iota(jnp.int32, (n, n), 0)
    c = lax.broadcasted_iota(jnp.int32, (n, n), 1)
    return (r <= c).astype(F32)


def _cumsum_kernel(x_ref, o_ref, *, chunk):
    t = x_ref.shape[1]
    u = _upper_ones(chunk)

    def body(i, carry):
        st = pl.multiple_of(i * chunk, chunk)
        seg = x_ref[:, pl.ds(st, chunk)]
        cs = jnp.dot(seg, u, preferred_element_type=F32, precision=lax.Precision.HIGHEST) + carry
        o_ref[:, pl.ds(st, chunk)] = cs
        return cs[:, chunk - 1:chunk]

    lax.fori_loop(0, t // chunk, body, jnp.zeros((x_ref.shape[0], 1), F32))


def _cumsum_t(xt):
    b, h, t = xt.shape
    chunk = 256 if t % 256 == 0 else t
    spec = pl.BlockSpec((None, h, t), lambda i: (i, 0, 0))
    return pl.pallas_call(functools.partial(_cumsum_kernel, chunk=chunk), grid=(b,), in_specs=[spec], out_specs=spec,
                          out_shape=jax.ShapeDtypeStruct(xt.shape, F32), compiler_params=_params(1),
                          name="cumsum")(xt)


def _online_update(s, v, m_ref, l_ref, acc_ref, idx):
    m_prev = m_ref[idx]
    m_next = jnp.maximum(m_prev, jnp.max(s, axis=1, keepdims=True))
    alpha = jnp.exp(m_prev - m_next)
    p = jnp.exp(s - m_next)
    l_ref[idx] = alpha * l_ref[idx] + jnp.sum(p, axis=1, keepdims=True)
    acc_ref[idx] = alpha * acc_ref[idx] + jnp.dot(p.astype(BF16), v, preferred_element_type=F32)
    m_ref[idx] = m_next


def _init_state(m_ref, l_ref, acc_ref):
    m_ref[...] = jnp.full(m_ref.shape, NEG, F32)
    l_ref[...] = jnp.zeros(l_ref.shape, F32)
    acc_ref[...] = jnp.zeros(acc_ref.shape, F32)


def _tri_tables(nq, kpq):
    qs, ks = [], []
    for qi in range(nq):
        for kj in range((qi + 1) * kpq):
            qs.append(qi)
            ks.append(kj)
    return jnp.asarray(np.asarray(qs, np.int32)), jnp.asarray(np.asarray(ks, np.int32))


def _diff_finalize(acc_ref, l_ref, lam_ref, sg_ref, g, lam_init):
    lam = lam_ref[...]
    lam_val = (jnp.exp(jnp.sum(lam[0:1] * lam[1:2], axis=1, keepdims=True))
               - jnp.exp(jnp.sum(lam[2:3] * lam[3:4], axis=1, keepdims=True)) + lam_init)
    o = acc_ref[g] / l_ref[g] - lam_val * (acc_ref[GROUP + g] / l_ref[GROUP + g])
    ms = jnp.mean(o * o, axis=-1, keepdims=True)
    return (o * lax.rsqrt(ms + RMS_EPS)) * sg_ref[...] * (1.0 - lam_init)


def _fox_kernel(qi_tab, kj_tab, q_ref, kv_ref, cq_ref, ck_ref, o_ref, m_ref, l_ref, acc_ref, *, tile):
    hk = pl.program_id(1)
    step = pl.program_id(2)
    qi, kj = qi_tab[step], kj_tab[step]

    @pl.when(kj == 0)
    def _():
        _init_state(m_ref, l_ref, acc_ref)

    def update(masked):
        k = kv_ref[:, :HEAD_DIM]
        v = kv_ref[:, HEAD_DIM:]
        for g in range(GROUP):
            s = _dot_t(q_ref[:, g * HEAD_DIM:(g + 1) * HEAD_DIM], k)
            s = s + (cq_ref[g][:, 0:1] - ck_ref[g])
            if masked:
                row = lax.broadcasted_iota(jnp.int32, (tile, tile), 0)
                col = lax.broadcasted_iota(jnp.int32, (tile, tile), 1)
                s = jnp.where(col <= row, s, NEG)
            _online_update(s, v, m_ref, l_ref, acc_ref, g)

    @pl.when(kj < qi)
    def _():
        update(False)

    @pl.when(kj == qi)
    def _():
        update(True)
        for g in range(GROUP):
            o_ref[:, g * HEAD_DIM:(g + 1) * HEAD_DIM] = (acc_ref[g] / l_ref[g]).astype(o_ref.dtype)


def _fox_prompt(q_all, kvbf, c_t, *, tile):
    b, t, _ = q_all.shape
    nq = t // tile
    qi_tab, kj_tab = _tri_tables(nq, 1)
    grid_spec = pltpu.PrefetchScalarGridSpec(
        num_scalar_prefetch=2, grid=(b, KV_B, int(qi_tab.shape[0])),
        in_specs=[
            pl.BlockSpec((None, tile, 2 * HEAD_DIM), lambda bi, h, s, qt, kt: (bi, qt[s], KV_A + h)),
            pl.BlockSpec((None, tile, 2 * HEAD_DIM), lambda bi, h, s, qt, kt: (bi, kt[s], KV_A + h)),
            pl.BlockSpec((None, GROUP, 1, tile), lambda bi, h, s, qt, kt: (bi, h, 0, qt[s])),
            pl.BlockSpec((None, GROUP, 1, tile), lambda bi, h, s, qt, kt: (bi, h, 0, kt[s])),
        ],
        out_specs=pl.BlockSpec((None, tile, 2 * HEAD_DIM), lambda bi, h, s, qt, kt: (bi, qt[s], h)),
        scratch_shapes=[pltpu.VMEM((GROUP, tile, 1), F32), pltpu.VMEM((GROUP, tile, 1), F32),
                        pltpu.VMEM((GROUP, tile, HEAD_DIM), F32)])
    return pl.pallas_call(
        functools.partial(_fox_kernel, tile=tile), grid_spec=grid_spec,
        out_shape=jax.ShapeDtypeStruct((b, t, W_BRANCH), BF16), compiler_params=_params(3),
        name="fox_prompt")(qi_tab, kj_tab, q_all, kvbf, c_t[:, :, None, :], c_t[:, :, None, :])


def _diff_kernel(qi_tab, kj_tab, slope_ref, q_ref, kv_ref, lam_ref, sg_ref, o_ref, m_ref, l_ref, acc_ref, *, tile,
                 lam_init):
    hk = pl.program_id(1)
    step = pl.program_id(2)
    qi, kj = qi_tab[step], kj_tab[step]

    @pl.when(kj == 0)
    def _():
        _init_state(m_ref, l_ref, acc_ref)

    def update(masked):
        v = kv_ref[:, 2 * HEAD_DIM:]
        rel = ((kj - qi) * tile + lax.broadcasted_iota(jnp.int32, (1, tile), 1)).astype(F32)
        if masked:
            row = lax.broadcasted_iota(jnp.int32, (tile, tile), 0)
            col = lax.broadcasted_iota(jnp.int32, (tile, tile), 1)
        for g in range(GROUP):
            bias = slope_ref[hk * GROUP + g] * rel
            for m in range(2):
                c = (m * GROUP + g) * HEAD_DIM
                s = _dot_t(q_ref[:, c:c + HEAD_DIM], kv_ref[:, m * HEAD_DIM:(m + 1) * HEAD_DIM]) + bias
                if masked:
                    s = jnp.where(col <= row, s, NEG)
                _online_update(s, v, m_ref, l_ref, acc_ref, m * GROUP + g)

    @pl.when(kj < qi)
    def _():
        update(False)

    @pl.when(kj == qi)
    def _():
        update(True)
        for g in range(GROUP):
            o = _diff_finalize(acc_ref, l_ref, lam_ref, sg_ref, g, lam_init)
            o_ref[:, g * V_DIM_C:(g + 1) * V_DIM_C] = o.astype(o_ref.dtype)


def _diff_prompt(q_all, kvbf, slopes, lam_l, sg_l, *, tile, lam_init):
    b, t, _ = q_all.shape
    nq = t // tile
    qi_tab, kj_tab = _tri_tables(nq, 1)
    wblk = 4 * HEAD_DIM
    grid_spec = pltpu.PrefetchScalarGridSpec(
        num_scalar_prefetch=2, grid=(b, KV_C, int(qi_tab.shape[0])),
        in_specs=[
            pl.BlockSpec(memory_space=pltpu.SMEM),
            pl.BlockSpec((None, tile, wblk), lambda bi, h, s, qt, kt: (bi, qt[s], 4 + h)),
            pl.BlockSpec((None, tile, wblk), lambda bi, h, s, qt, kt: (bi, kt[s], 4 + h)),
            pl.BlockSpec((4, HEAD_DIM), lambda bi, h, s, qt, kt: (0, 0)),
            pl.BlockSpec((1, V_DIM_C), lambda bi, h, s, qt, kt: (0, 0)),
        ],
        out_specs=pl.BlockSpec((None, tile, wblk), lambda bi, h, s, qt, kt: (bi, qt[s], h)),
        scratch_shapes=[pltpu.VMEM((2 * GROUP, tile, 1), F32), pltpu.VMEM((2 * GROUP, tile, 1), F32),
                        pltpu.VMEM((2 * GROUP, tile, V_DIM_C), F32)])
    return pl.pallas_call(
        functools.partial(_diff_kernel, tile=tile, lam_init=lam_init), grid_spec=grid_spec,
        out_shape=jax.ShapeDtypeStruct((b, t, W_BRANCH), BF16), compiler_params=_params(3),
        name="diff_prompt")(qi_tab, kj_tab, slopes, q_all, kvbf, lam_l, sg_l)


def _topk_bias(gate, own, nblk):
    lane = lax.broadcasted_iota(jnp.int32, (1, LANES), 1)
    past = lane < own
    gm = jnp.where(past, gate, -jnp.inf)
    cnt = jnp.zeros(gate.shape, jnp.int32)
    for jp in range(nblk):
        gj = gm[:, jp:jp + 1]
        ahead = (gj > gm) | ((gj == gm) & (jp < lane))
        cnt = cnt + ahead.astype(jnp.int32)
    keep = (past & (cnt < MOBA_TOPK)) | (lane == own)
    return jnp.where(keep, 0.0, NEG)


def _moba_kernel(qi_tab, kj_tab, slope_ref, q_ref, kv_ref, km_ref, o_ref, m_ref, l_ref, acc_ref, selb_ref, *, tq,
                 nblk):
    hk = pl.program_id(1)
    step = pl.program_id(2)
    qi, kj = qi_tab[step], kj_tab[step]
    kpq = tq // MOBA_BLOCK
    rowpos = qi * tq + lax.broadcasted_iota(jnp.int32, (tq, 1), 0)

    @pl.when(kj == 0)
    def _():
        _init_state(m_ref, l_ref, acc_ref)
        own = lax.shift_right_logical(rowpos, int(math.log2(MOBA_BLOCK)))
        for g in range(GROUP):
            qf = q_ref[:, g * HEAD_DIM:(g + 1) * HEAD_DIM].astype(F32)
            gate = _dot_t(qf, km_ref[:, :HEAD_DIM], precision=lax.Precision.HIGHEST)
            selb_ref[g] = _topk_bias(gate, own, nblk)

    def update(masked):
        k = kv_ref[:, :HEAD_DIM]
        v = kv_ref[:, HEAD_DIM:]
        kpos = kj * MOBA_BLOCK + lax.broadcasted_iota(jnp.int32, (1, MOBA_BLOCK), 1)
        rel = (kpos - qi * tq).astype(F32)
        lane = lax.broadcasted_iota(jnp.int32, (1, LANES), 1)
        for g in range(GROUP):
            rowb = jnp.sum(jnp.where(lane == kj, selb_ref[g], 0.0), axis=1, keepdims=True)
            s = _dot_t(q_ref[:, g * HEAD_DIM:(g + 1) * HEAD_DIM], k)
            s = s + slope_ref[hk * GROUP + g] * rel + rowb
            if masked:
                s = jnp.where(kpos <= rowpos, s, NEG)
            _online_update(s, v, m_ref, l_ref, acc_ref, g)

    @pl.when(kj < qi * kpq)
    def _():
        update(False)

    @pl.when(kj >= qi * kpq)
    def _():
        update(True)

    @pl.when(kj == (qi + 1) * kpq - 1)
    def _():
        for g in range(GROUP):
            o_ref[:, g * HEAD_DIM:(g + 1) * HEAD_DIM] = (acc_ref[g] / l_ref[g]).astype(o_ref.dtype)


def _moba_prompt(q_all, kvbf, kmean_pad, slopes, *, tq):
    b, t, _ = q_all.shape
    nq = t // tq
    kpq = tq // MOBA_BLOCK
    qi_tab, kj_tab = _tri_tables(nq, kpq)
    grid_spec = pltpu.PrefetchScalarGridSpec(
        num_scalar_prefetch=2, grid=(b, KV_A, int(qi_tab.shape[0])),
        in_specs=[
            pl.BlockSpec(memory_space=pltpu.SMEM),
            pl.BlockSpec((None, tq, 2 * HEAD_DIM), lambda bi, h, s, qt, kt: (bi, qt[s], h)),
            pl.BlockSpec((None, MOBA_BLOCK, 2 * HEAD_DIM), lambda bi, h, s, qt, kt: (bi, kt[s], h)),
            pl.BlockSpec((None, LANES, 2 * HEAD_DIM), lambda bi, h, s, qt, kt: (bi, 0, h)),
        ],
        out_specs=pl.BlockSpec((None, tq, 2 * HEAD_DIM), lambda bi, h, s, qt, kt: (bi, qt[s], h)),
        scratch_shapes=[pltpu.VMEM((GROUP, tq, 1), F32), pltpu.VMEM((GROUP, tq, 1), F32),
                        pltpu.VMEM((GROUP, tq, HEAD_DIM), F32), pltpu.VMEM((GROUP, tq, LANES), F32)])
    return pl.pallas_call(
        functools.partial(_moba_kernel, tq=tq, nblk=t // MOBA_BLOCK), grid_spec=grid_spec,
        out_shape=jax.ShapeDtypeStruct((b, t, W_BRANCH), BF16), compiler_params=_params(3),
        name="moba_prompt")(qi_tab, kj_tab, slopes, q_all, kvbf, kmean_pad)


def _stack_heads(q_ref, col, n_tok):
    return jnp.concatenate([q_ref[:, col + g * HEAD_DIM: col + (g + 1) * HEAD_DIM] for g in range(GROUP)], axis=0)


def _new_token_mask(n_tok):
    row = lax.broadcasted_iota(jnp.int32, (GROUP * n_tok, n_tok), 0)
    col = lax.broadcasted_iota(jnp.int32, (GROUP * n_tok, n_tok), 1)
    t = jnp.where(row >= n_tok, row - n_tok, row)
    return col <= t


def _rows_per_head(vals, n_tok, width):
    return jnp.concatenate([jnp.broadcast_to(v, (n_tok, width)) for v in vals], axis=0)


def _fox_sample_kernel(pt_ref, q_ref, kvn_ref, lfn_ref, *rest, n_pages_step, n_tok):
    pages = rest[:n_pages_step]
    lfs = rest[n_pages_step:2 * n_pages_step]
    o_ref, m_ref, l_ref, acc_ref, carry_ref = rest[2 * n_pages_step:]
    j = pl.program_id(1)
    rows = GROUP * n_tok

    @pl.when(j == 0)
    def _():
        _init_state(m_ref, l_ref, acc_ref)
        carry_ref[...] = jnp.zeros(carry_ref.shape, F32)

    def c_rows(lf, carry):
        n = lf.shape[0]
        return lax.dot_general(lf, _upper_ones(n), (((0,), (0,)), ((), ())), preferred_element_type=F32,
                               precision=lax.Precision.HIGHEST) + carry

    def attend(q_hk, kv, c_t, hk, mask):
        k = kv[:, hk * 2 * HEAD_DIM: hk * 2 * HEAD_DIM + HEAD_DIM].astype(BF16)
        v = kv[:, hk * 2 * HEAD_DIM + HEAD_DIM: (hk + 1) * 2 * HEAD_DIM].astype(BF16)
        nk = kv.shape[0]
        bias = _rows_per_head([-c_t[hk * GROUP + g: hk * GROUP + g + 1, :] for g in range(GROUP)], n_tok, nk)
        s = _dot_t(q_hk, k) + bias
        if mask is not None:
            s = jnp.where(mask, s, NEG)
        _online_update(s, v, m_ref, l_ref, acc_ref, hk)

    qs = [_stack_heads(q_ref, hk * 2 * HEAD_DIM, n_tok).astype(BF16) for hk in range(KV_B)]
    carry = carry_ref[...]
    for r in range(n_pages_step):
        c_t = c_rows(lfs[r][...], carry)
        carry = c_t[:, PAGE_SIZE - 1:PAGE_SIZE]
        kv = pages[r][...]
        for hk in range(KV_B):
            attend(qs[hk], kv, c_t, hk, None)
    carry_ref[...] = carry

    @pl.when(j == pl.num_programs(1) - 1)
    def _():
        c_new = c_rows(lfn_ref[...], carry)
        mask = _new_token_mask(n_tok)
        kvn = kvn_ref[...]
        for hk in range(KV_B):
            attend(qs[hk], kvn, c_new, hk, mask)
            o = acc_ref[hk] / l_ref[hk]
            for g in range(GROUP):
                c = (hk * GROUP + g) * HEAD_DIM
                o_ref[:, c:c + HEAD_DIM] = o[g * n_tok:(g + 1) * n_tok]


def _diff_sample_kernel(pt_ref, slope_ref, lam_ref, sg_ref, q_ref, kvn_ref, *rest, n_pages_step, n_tok, lam_init):
    pages = rest[:n_pages_step]
    o_ref, m_ref, l_ref, acc_ref = rest[n_pages_step:]
    j = pl.program_id(1)
    wrow = 2 * HEAD_DIM + V_DIM_C

    @pl.when(j == 0)
    def _():
        _init_state(m_ref, l_ref, acc_ref)

    def attend(kv, rel, hk, mask):
        nk = kv.shape[0]
        v = kv[:, hk * wrow + 2 * HEAD_DIM:(hk + 1) * wrow].astype(BF16)
        bias = _rows_per_head([slope_ref[hk * GROUP + g] * rel for g in range(GROUP)], n_tok, nk)
        for m in range(2):
            q = _stack_heads(q_ref, (hk * 2 + m) * GROUP * HEAD_DIM, n_tok).astype(BF16)
            k = kv[:, hk * wrow + m * HEAD_DIM: hk * wrow + (m + 1) * HEAD_DIM].astype(BF16)
            s = _dot_t(q, k) + bias
            if mask is not None:
                s = jnp.where(mask, s, NEG)
            _online_update(s, v, m_ref, l_ref, acc_ref, hk * 2 + m)

    n_past = pl.num_programs(1) * n_pages_step * PAGE_SIZE
    for r in range(n_pages_step):
        kpos = (j * n_pages_step + r) * PAGE_SIZE + lax.broadcasted_iota(jnp.int32, (1, PAGE_SIZE), 1)
        rel = (kpos - n_past).astype(F32)
        kv = pages[r][...]
        for hk in range(KV_C):
            attend(kv, rel, hk, None)

    @pl.when(j == pl.num_programs(1) - 1)
    def _():
        rel = lax.broadcasted_iota(jnp.int32, (1, n_tok), 1).astype(F32)
        mask = _new_token_mask(n_tok)
        kvn = kvn_ref[...]
        lam = lam_ref[...]
        lam_val = (jnp.exp(jnp.sum(lam[0:1] * lam[1:2], axis=1, keepdims=True))
                   - jnp.exp(jnp.sum(lam[2:3] * lam[3:4], axis=1, keepdims=True)) + lam_init)
        for hk in range(KV_C):
            attend(kvn, rel, hk, mask)
            o = acc_ref[hk * 2] / l_ref[hk * 2] - lam_val * (acc_ref[hk * 2 + 1] / l_ref[hk * 2 + 1])
            ms = jnp.mean(o * o, axis=-1, keepdims=True)
            o = (o * lax.rsqrt(ms + RMS_EPS)) * sg_ref[...] * (1.0 - lam_init)
            for g in range(GROUP):
                c = (hk * GROUP + g) * V_DIM_C
                o_ref[:, c:c + V_DIM_C] = o[g * n_tok:(g + 1) * n_tok]


def _moba_sample_kernel(pt_ref, slope_ref, q_ref, kvn_ref, *rest, n_pages_step, n_tok, nblk):
    pages = rest[:n_pages_step]
    o_ref, mb_ref, lb_ref, gb_ref, ob_ref = rest[n_pages_step:]
    j = pl.program_id(1)
    rows = GROUP * n_tok
    ppb = MOBA_BLOCK // PAGE_SIZE
    bps = n_pages_step // ppb
    lane = lax.broadcasted_iota(jnp.int32, (1, LANES), 1)
    n_past = pl.num_programs(1) * n_pages_step * PAGE_SIZE

    @pl.when(j == 0)
    def _():
        mb_ref[...] = jnp.full(mb_ref.shape, NEG, F32)
        lb_ref[...] = jnp.zeros(lb_ref.shape, F32)
        gb_ref[...] = jnp.zeros(gb_ref.shape, F32)

    qs = [_stack_heads(q_ref, hk * 2 * HEAD_DIM, n_tok) for hk in range(KV_A)]

    def block_logits(q_hk, k, rel, hk):
        bias = _rows_per_head([slope_ref[hk * GROUP + g] * rel for g in range(GROUP)], n_tok, k.shape[0])
        return _dot_t(q_hk.astype(BF16), k.astype(BF16)) + bias

    for bl in range(bps):
        blk = j * bps + bl
        kv = jnp.concatenate([pages[bl * ppb + r][...] for r in range(ppb)], axis=0)
        kpos = blk * MOBA_BLOCK + lax.broadcasted_iota(jnp.int32, (1, MOBA_BLOCK), 1)
        rel = (kpos - n_past).astype(F32)
        for hk in range(KV_A):
            k = kv[:, hk * 2 * HEAD_DIM: hk * 2 * HEAD_DIM + HEAD_DIM]
            v = kv[:, hk * 2 * HEAD_DIM + HEAD_DIM:(hk + 1) * 2 * HEAD_DIM].astype(BF16)
            s = block_logits(qs[hk], k, rel, hk)
            mb = jnp.max(s, axis=1, keepdims=True)
            p = jnp.exp(s - mb)
            lb = jnp.sum(p, axis=1, keepdims=True)
            ob_ref[hk, blk] = jnp.dot(p.astype(BF16), v, preferred_element_type=F32)
            gate = jnp.sum(qs[hk] * jnp.mean(k, axis=0, keepdims=True), axis=1, keepdims=True)
            here = lane == blk
            mb_ref[hk] = jnp.where(here, mb, mb_ref[hk])
            lb_ref[hk] = jnp.where(here, lb, lb_ref[hk])
            gb_ref[hk] = jnp.where(here, gate, gb_ref[hk])

    @pl.when(j == pl.num_programs(1) - 1)
    def _():
        own = jnp.full((rows, 1), nblk, jnp.int32)
        rel = lax.broadcasted_iota(jnp.int32, (1, n_tok), 1).astype(F32)
        mask = _new_token_mask(n_tok)
        kvn = kvn_ref[...]
        for hk in range(KV_A):
            selb = _topk_bias(gb_ref[hk], own, nblk)
            m_sel = mb_ref[hk] + selb
            kn = kvn[:, hk * 2 * HEAD_DIM: hk * 2 * HEAD_DIM + HEAD_DIM]
            vn = kvn[:, hk * 2 * HEAD_DIM + HEAD_DIM:(hk + 1) * 2 * HEAD_DIM].astype(BF16)
            s = jnp.where(mask, block_logits(qs[hk], kn, rel, hk), NEG)
            m_tot = jnp.maximum(jnp.max(s, axis=1, keepdims=True), jnp.max(m_sel, axis=1, keepdims=True))
            p = jnp.exp(s - m_tot)
            w = jnp.where(selb == 0.0, jnp.exp(m_sel - m_tot), 0.0)
            den = jnp.sum(p, axis=1, keepdims=True) + jnp.sum(w * lb_ref[hk], axis=1, keepdims=True)
            num = jnp.dot(p.astype(BF16), vn, preferred_element_type=F32)
            for blk in range(nblk):
                num = num + w[:, blk:blk + 1] * ob_ref[hk, blk]
            o = num / den
            for g in range(GROUP):
                c = (hk * GROUP + g) * HEAD_DIM
                o_ref[:, c:c + HEAD_DIM] = o[g * n_tok:(g + 1) * n_tok]


def _page_specs(layer, n_pages, n_pages_step, width):
    def spec(r):
        return pl.BlockSpec((None, None, PAGE_SIZE, width),
                            lambda b, j, pt: (layer, pt[b * n_pages + j * n_pages_step + r], 0, 0))
    return [spec(r) for r in range(n_pages_step)]


def _sample_attention(kind, layer, pt_flat, q_s, kvn, cache, *, n_pages, extra=(), extra_specs=(), logf_cache=None,
                      lfn=None, lam_init=0.0):
    db, n_tok, _ = q_s.shape
    nps = 8 if n_pages % 8 == 0 else 2
    rows = GROUP * n_tok
    per_b = lambda w: pl.BlockSpec((None, n_tok, w), lambda b, j, pt: (b, 0, 0))
    qcol = {"moba": 0, "fox": 1, "diff": 2}[kind]
    q_spec = pl.BlockSpec((None, n_tok, W_BRANCH), lambda b, j, pt: (b, 0, qcol))
    in_specs = list(extra_specs) + [q_spec, per_b(W_BRANCH)]
    args = list(extra) + [q_s, kvn]
    if kind == "fox":
        in_specs.append(per_b(H_B))
        args.append(lfn)
    in_specs += _page_specs(layer, n_pages, nps, W_BRANCH)
    args += [cache] * nps
    if kind == "fox":
        in_specs += _page_specs(layer, n_pages, nps, H_B)
        args += [logf_cache] * nps
        kern = functools.partial(_fox_sample_kernel, n_pages_step=nps, n_tok=n_tok)
        scratch = [pltpu.VMEM((KV_B, rows, 1), F32), pltpu.VMEM((KV_B, rows, 1), F32),
                   pltpu.VMEM((KV_B, rows, HEAD_DIM), F32), pltpu.VMEM((H_B, 1), F32)]
    elif kind == "diff":
        kern = functools.partial(_diff_sample_kernel, n_pages_step=nps, n_tok=n_tok, lam_init=lam_init)
        scratch = [pltpu.VMEM((2 * KV_C, rows, 1), F32), pltpu.VMEM((2 * KV_C, rows, 1), F32),
                   pltpu.VMEM((2 * KV_C, rows, V_DIM_C), F32)]
    else:
        nblk = n_pages * PAGE_SIZE // MOBA_BLOCK
        kern = functools.partial(_moba_sample_kernel, n_pages_step=nps, n_tok=n_tok, nblk=nblk)
        scratch = [pltpu.VMEM((KV_A, rows, LANES), F32), pltpu.VMEM((KV_A, rows, LANES), F32),
                   pltpu.VMEM((KV_A, rows, LANES), F32), pltpu.VMEM((KV_A, nblk, rows, HEAD_DIM), F32)]
    grid_spec = pltpu.PrefetchScalarGridSpec(
        num_scalar_prefetch=1, grid=(db, n_pages // nps), in_specs=in_specs, out_specs=per_b(W_BRANCH),
        scratch_shapes=scratch)
    return pl.pallas_call(kern, grid_spec=grid_spec, out_shape=jax.ShapeDtypeStruct((db, n_tok, W_BRANCH), F32),
                          compiler_params=_params(2), name=kind + "_sample")(pt_flat, *args)


def _alibi_slopes(n):
    return 2.0 ** (-8.0 * jnp.arange(1, n + 1, dtype=F32) / n)


def kernel(x_prompt, x_sample, cache_a_kv, cache_b_kv, cache_b_logf, cache_c_kv, page_table, w_in, w_o_a, w_o_b,
           w_o_c, w_out, norm_g, forget_b, diff_lambda, diff_subln_g, final_norm_g):
    depth = w_in.shape[0]
    bp, t, _ = x_prompt.shape
    db, n_tok, _ = x_sample.shape
    n_pages = page_table.shape[1]
    n_pool = cache_a_kv.shape[1]
    assert t % MOBA_BLOCK == 0 and (n_pages * PAGE_SIZE) % MOBA_BLOCK == 0 and n_tok <= MOBA_BLOCK
    assert t // MOBA_BLOCK <= LANES and n_pages * PAGE_SIZE // MOBA_BLOCK <= LANES

    w_pad = jnp.pad(w_in, ((0, 0), (0, 0), (0, 1)))
    w_perm = (jnp.take(w_pad, jnp.asarray(_PERM), axis=2) * jnp.asarray(_SCALE)).astype(BF16)
    woa, wob, woc, wout = (w.astype(BF16) for w in (w_o_a, w_o_b, w_o_c, w_out))
    norm_g3 = norm_g.reshape(depth, 1, D_MODEL)
    fb3 = jnp.pad(forget_b, ((0, 0), (0, FB_PAD - H_B))).reshape(depth, 1, FB_PAD)
    fg2 = final_norm_g.reshape(1, D_MODEL)
    slopes_a, slopes_c = _alibi_slopes(H_A), _alibi_slopes(H_C)
    pt_flat = page_table.reshape(-1).astype(jnp.int32)
    ca = cache_a_kv.reshape(depth, n_pool, PAGE_SIZE, W_BRANCH)
    cb = cache_b_kv.reshape(depth, n_pool, PAGE_SIZE, W_BRANCH)
    cc = cache_c_kv.reshape(depth, n_pool, PAGE_SIZE, W_BRANCH)

    tm_p = 512 if (bp * t) % 512 == 0 else bp * t
    tile = 512 if t % 512 == 0 else t
    ns = db * n_tok
    smem = pl.BlockSpec(memory_space=pltpu.SMEM)

    xp = x_prompt.reshape(bp * t, D_MODEL)
    xs = x_sample.reshape(ns, D_MODEL)
    outs_p = [[], [], [], []]
    outs_s = [[], [], [], []]
    for l in range(depth):
        lam_init = 0.8 - 0.6 * math.exp(-0.3 * l)
        sg = diff_subln_g[l].reshape(1, V_DIM_C)
        final = l == depth - 1

        kva, kvb, kvc, logf, kvbf, q_all, gates, gm = _inproj(xp, l, norm_g3, fb3, w_perm, tm=tm_p, q_dtype=BF16)
        for acc, v in zip(outs_p, (kva, kvb, logf, kvc)):
            acc.append(v)
        q3 = q_all.reshape(bp, t, 3 * W_BRANCH)
        kv3 = kvbf.reshape(bp, t, 3 * W_BRANCH)
        km = _kmean(kva).reshape(bp, t // MOBA_BLOCK, W_BRANCH)
        km = jnp.pad(km, ((0, 0), (0, LANES - t // MOBA_BLOCK), (0, 0)))
        c_t = _cumsum_t(jnp.swapaxes(logf.reshape(bp, t, H_B), 1, 2))
        oa = _moba_prompt(q3, kv3, km, slopes_a, tq=tile)
        ob = _fox_prompt(q3, kv3, c_t, tile=tile)
        oc = _diff_prompt(q3, kv3, slopes_c, diff_lambda[l], sg, tile=tile, lam_init=lam_init)
        xp = _outproj(xp, oa.reshape(bp * t, W_BRANCH), ob.reshape(bp * t, W_BRANCH), oc.reshape(bp * t, W_BRANCH),
                      gates, gm, l, woa, wob, woc, wout, fg2, tm=tm_p, final=final)

        kva, kvb, kvc, logf, _, q_all, gates, gm = _inproj(xs, l, norm_g3, fb3, w_perm, tm=ns, q_dtype=F32)
        for acc, v in zip(outs_s, (kva, kvb, logf, kvc)):
            acc.append(v)
        q3 = q_all.reshape(db, n_tok, 3 * W_BRANCH)
        r3 = lambda a: a.reshape(db, n_tok, a.shape[-1])
        oa = _sample_attention("moba", l, pt_flat, q3, r3(kva), ca, n_pages=n_pages, extra=(slopes_a,),
                               extra_specs=(smem,))
        ob = _sample_attention("fox", l, pt_flat, q3, r3(kvb), cb, n_pages=n_pages, logf_cache=cache_b_logf,
                               lfn=r3(logf))
        oc = _sample_attention("diff", l, pt_flat, q3, r3(kvc), cc, n_pages=n_pages,
                               extra=(slopes_c, diff_lambda[l], sg),
                               extra_specs=(smem, pl.BlockSpec((4, HEAD_DIM), lambda b, j, pt: (0, 0)),
                                            pl.BlockSpec((1, V_DIM_C), lambda b, j, pt: (0, 0))),
                               lam_init=lam_init)
        xs = _outproj(xs, oa.reshape(ns, W_BRANCH), ob.reshape(ns, W_BRANCH), oc.reshape(ns, W_BRANCH), gates, gm, l,
                      woa, wob, woc, wout, fg2, tm=ns, final=final)

    def stack(vals, b_, t_, tail):
        return jnp.stack(vals).reshape((depth, b_, t_) + tail)

    pa, pb, pf, pc = outs_p
    sa, sb, sf, sc = outs_s
    kv_tail, c_tail = (KV_A, 2 * HEAD_DIM), (KV_C, 2 * HEAD_DIM + V_DIM_C)
    return (xp.reshape(bp, t, D_MODEL), xs.reshape(db, n_tok, D_MODEL),
            stack(pa, bp, t, kv_tail), stack(pb, bp, t, kv_tail), stack(pf, bp, t, (H_B,)), stack(pc, bp, t, c_tail),
            stack(sa, db, n_tok, kv_tail), stack(sb, db, n_tok, kv_tail), stack(sf, db, n_tok, (H_B,)),
            stack(sc, db, n_tok, c_tail))
```

```python
import functools
import math

import jax
import jax.numpy as jnp
import numpy as np
from jax import lax
from jax.experimental import pallas as pl
from jax.experimental.pallas import tpu as pltpu

F32 = jnp.float32
BF16 = jnp.bfloat16

D_MODEL = 1024
HEAD_DIM = 64
PAGE_SIZE = 128
H_A, KV_A = 8, 4
H_B, KV_B = 8, 4
H_C, KV_C = 4, 2
GROUP = 2
V_DIM_C = 2 * HEAD_DIM
MOBA_BLOCK = 256
MOBA_TOPK = 3
N_BRANCH = 3
RMS_EPS = 1e-6
W_BRANCH = 512
IN_NAMES = ("qa", "ka", "va", "ga", "qb", "kb", "vb", "fb", "gb", "qc", "kc", "vc", "gc", "gm")
IN_SIZES = (512, 256, 256, 512, 512, 256, 256, 8, 512, 512, 256, 256, 512, N_BRANCH * D_MODEL)
D_IN = sum(IN_SIZES)

NEG = -1e30
LOG2E = math.log2(math.e)
LANES = 128
SUBLANES = 8
FB_PAD = LANES
OFF_KVA, OFF_KVB, OFF_KVC, OFF_FB = 0, 512, 1024, 1536
OFF_Q = OFF_FB + FB_PAD
OFF_G = OFF_Q + 3 * W_BRANCH
OFF_GM = OFF_G + 3 * W_BRANCH
D_PERM = OFF_GM + N_BRANCH * D_MODEL
KVX_AB = 4 * HEAD_DIM
KVX_C = 8 * HEAD_DIM
OFF_XA, OFF_XB, OFF_XC = 0, KV_A * KVX_AB, KV_A * KVX_AB + KV_B * KVX_AB
W_KVX = OFF_XC + KV_C * KVX_C
VMEM_LIMIT = 56 * 1024 * 1024
PAGES_PER_STEP = 16


def _perm_and_scale():
    off = dict(zip(IN_NAMES, np.concatenate([[0], np.cumsum(IN_SIZES)[:-1]])))
    d = HEAD_DIM
    cols = []
    for kn, vn, nkv in (("ka", "va", KV_A), ("kb", "vb", KV_B)):
        for h in range(nkv):
            cols += list(range(off[kn] + h * d, off[kn] + (h + 1) * d))
            cols += list(range(off[vn] + h * d, off[vn] + (h + 1) * d))
    for h in range(KV_C):
        for m in range(2):
            cols += list(range(off["kc"] + (m * KV_C + h) * d, off["kc"] + (m * KV_C + h + 1) * d))
        cols += list(range(off["vc"] + h * V_DIM_C, off["vc"] + (h + 1) * V_DIM_C))
    cols += list(range(off["fb"], off["fb"] + H_B)) + [D_IN] * (FB_PAD - H_B)
    cols += list(range(off["qa"], off["qa"] + 512))
    cols += list(range(off["qb"], off["qb"] + 512))
    for h in range(KV_C):
        for m in range(2):
            for g in range(GROUP):
                s = off["qc"] + ((m * KV_C + h) * GROUP + g) * d
                cols += list(range(s, s + d))
    for n in ("ga", "gb", "gc"):
        cols += list(range(off[n], off[n] + 512))
    cols += list(range(off["gm"], off["gm"] + N_BRANCH * D_MODEL))
    perm = np.asarray(cols, np.int32)
    assert perm.shape[0] == D_PERM
    scale = np.ones((D_PERM,), np.float32)
    scale[OFF_Q:OFF_G] = HEAD_DIM ** -0.5 * LOG2E
    return perm, scale


_PERM, _SCALE = _perm_and_scale()


def _params(n_axes):
    return pltpu.CompilerParams(dimension_semantics=("arbitrary",) * n_axes, vmem_limit_bytes=VMEM_LIMIT)


def _sigmoid(x):
    return 1.0 / (1.0 + jnp.exp(-x))


def _dot_t(a, b, precision=None):
    return lax.dot_general(a, b, (((1,), (1,)), ((), ())), preferred_element_type=F32, precision=precision)


def _dot_tl(a, b):
    return lax.dot_general(a, b, (((0,), (0,)), ((), ())), preferred_element_type=F32,
                           precision=lax.Precision.HIGHEST)


def _iota(shape, axis):
    return lax.broadcasted_iota(jnp.int32, shape, axis)


def _log2(n):
    assert n & (n - 1) == 0
    return n.bit_length() - 1


def _inproj_kernel(x_ref, g_ref, fb_ref, w_ref, kva_ref, kvb_ref, kvc_ref, logf_ref, kvx_ref, q_ref, gate_ref,
                   gm_ref):
    x = x_ref[...]
    ms = jnp.mean(x * x, axis=-1, keepdims=True)
    h = ((x * lax.rsqrt(ms + RMS_EPS)) * g_ref[...]).astype(BF16)

    def mm(a, n):
        return jnp.dot(h, w_ref[:, a:a + n], preferred_element_type=F32)

    low = (_iota((1, W_BRANCH), 1) & (LANES - 1)) < HEAD_DIM
    for i, (ref, off_x) in enumerate(((kva_ref, OFF_XA), (kvb_ref, OFF_XB))):
        z = mm(i * W_BRANCH, W_BRANCH)
        ref[...] = z
        kx = jnp.where(low, z, 0.0).astype(BF16)
        vx = jnp.where(low, pltpu.roll(z, W_BRANCH - HEAD_DIM, axis=1), 1.0).astype(BF16)
        for hk in range(KV_A):
            kvx_ref[:, off_x + hk * KVX_AB: off_x + hk * KVX_AB + LANES] = kx[:, hk * LANES:(hk + 1) * LANES]
            kvx_ref[:, off_x + hk * KVX_AB + LANES: off_x + (hk + 1) * KVX_AB] = vx[:, hk * LANES:(hk + 1) * LANES]
    z = mm(OFF_KVC, W_BRANCH)
    kvc_ref[...] = z
    k1 = jnp.where(low, z, 0.0).astype(BF16)
    k2 = jnp.where(low, pltpu.roll(z, W_BRANCH - HEAD_DIM, axis=1), 0.0).astype(BF16)
    for hk in range(KV_C):
        c, o = hk * 2 * LANES, OFF_XC + hk * KVX_C
        kvx_ref[:, o: o + LANES] = k1[:, c:c + LANES]
        kvx_ref[:, o + LANES: o + 2 * LANES] = k2[:, c:c + LANES]
        kvx_ref[:, o + 2 * LANES: o + 3 * LANES] = z[:, c + LANES: c + 2 * LANES].astype(BF16)
        kvx_ref[:, o + 3 * LANES: o + 4 * LANES] = jnp.ones((z.shape[0], LANES), BF16)
    zf = mm(OFF_FB, FB_PAD) + fb_ref[...]
    lf = jnp.minimum(zf, 0.0) - jnp.log(1.0 + jnp.exp(-jnp.abs(zf)))
    logf_ref[...] = lf[:, :H_B]
    for i in range(3):
        q_ref[:, i * W_BRANCH:(i + 1) * W_BRANCH] = mm(OFF_Q + i * W_BRANCH, W_BRANCH).astype(q_ref.dtype)
        gate_ref[:, i * W_BRANCH:(i + 1) * W_BRANCH] = mm(OFF_G + i * W_BRANCH, W_BRANCH).astype(BF16)
    for i in range(N_BRANCH * D_MODEL // W_BRANCH):
        gm_ref[:, i * W_BRANCH:(i + 1) * W_BRANCH] = mm(OFF_GM + i * W_BRANCH, W_BRANCH).astype(BF16)


def _inproj(x2d, layer, norm_g3, fb3, w_perm, *, tm, q_dtype):
    n = x2d.shape[0]
    row = lambda w: pl.BlockSpec((tm, w), lambda i: (i, 0))
    lay = lambda shp: pl.BlockSpec((None,) + shp, lambda i: (layer, 0, 0))
    out_shape = (
        jax.ShapeDtypeStruct((n, W_BRANCH), F32), jax.ShapeDtypeStruct((n, W_BRANCH), F32),
        jax.ShapeDtypeStruct((n, W_BRANCH), F32), jax.ShapeDtypeStruct((n, H_B), F32),
        jax.ShapeDtypeStruct((n, W_KVX), BF16), jax.ShapeDtypeStruct((n, 3 * W_BRANCH), q_dtype),
        jax.ShapeDtypeStruct((n, 3 * W_BRANCH), BF16), jax.ShapeDtypeStruct((n, N_BRANCH * D_MODEL), BF16))
    return pl.pallas_call(
        _inproj_kernel, grid=(n // tm,),
        in_specs=[row(D_MODEL), lay((1, D_MODEL)), lay((1, FB_PAD)), lay((D_MODEL, D_PERM))],
        out_specs=[row(W_BRANCH), row(W_BRANCH), row(W_BRANCH), row(H_B), row(W_KVX), row(3 * W_BRANCH),
                   row(3 * W_BRANCH), row(N_BRANCH * D_MODEL)],
        out_shape=out_shape, compiler_params=_params(1), name="inproj")(x2d, norm_g3, fb3, w_perm)


def _outproj_kernel(x_ref, oa_ref, ob_ref, oc_ref, gate_ref, gm_ref, woa_ref, wob_ref, woc_ref, wout_ref, fg_ref,
                    y_ref, *, final):
    def branch(i, o_ref, w_ref):
        g = gate_ref[:, i * W_BRANCH:(i + 1) * W_BRANCH].astype(F32)
        u = (o_ref[...].astype(F32) * (g * _sigmoid(g))).astype(BF16)
        p = jnp.dot(u, w_ref[...], preferred_element_type=F32)
        return _sigmoid(gm_ref[:, i * D_MODEL:(i + 1) * D_MODEL].astype(F32)) * p

    merged = branch(0, oa_ref, woa_ref) + branch(1, ob_ref, wob_ref) + branch(2, oc_ref, woc_ref)
    y = x_ref[...] + jnp.dot(merged.astype(BF16), wout_ref[...], preferred_element_type=F32)
    if final:
        ms = jnp.mean(y * y, axis=-1, keepdims=True)
        y = (y * lax.rsqrt(ms + RMS_EPS)) * fg_ref[...]
    y_ref[...] = y


def _outproj(x2d, oa, ob, oc, gates, gm, layer, woa, wob, woc, wout, fg2, *, tm, final):
    n = x2d.shape[0]
    row = lambda w: pl.BlockSpec((tm, w), lambda i: (i, 0))
    lay = lambda shp: pl.BlockSpec((None,) + shp, lambda i: (layer, 0, 0))
    return pl.pallas_call(
        functools.partial(_outproj_kernel, final=final), grid=(n // tm,),
        in_specs=[row(D_MODEL), row(W_BRANCH), row(W_BRANCH), row(W_BRANCH), row(3 * W_BRANCH),
                  row(N_BRANCH * D_MODEL), lay((W_BRANCH, D_MODEL)), lay((W_BRANCH, D_MODEL)),
                  lay((W_BRANCH, D_MODEL)), lay((D_MODEL, D_MODEL)), pl.BlockSpec((1, D_MODEL), lambda i: (0, 0))],
        out_specs=row(D_MODEL), out_shape=jax.ShapeDtypeStruct((n, D_MODEL), F32),
        compiler_params=_params(1), name="outproj")(x2d, oa, ob, oc, gates, gm, woa, wob, woc, wout, fg2)


def _kmean_kernel(kv_ref, o_ref, *, nb):
    for i in range(nb):
        o_ref[i:i + 1, :] = jnp.mean(kv_ref[i * MOBA_BLOCK:(i + 1) * MOBA_BLOCK, :], axis=0, keepdims=True)


def _kmean(kva2d):
    nblk = kva2d.shape[0] // MOBA_BLOCK
    nb = SUBLANES if nblk % SUBLANES == 0 else nblk
    return pl.pallas_call(
        functools.partial(_kmean_kernel, nb=nb), grid=(nblk // nb,),
        in_specs=[pl.BlockSpec((nb * MOBA_BLOCK, W_BRANCH), lambda i: (i, 0))],
        out_specs=pl.BlockSpec((nb, W_BRANCH), lambda i: (i, 0)),
        out_shape=jax.ShapeDtypeStruct((nblk, W_BRANCH), F32), compiler_params=_params(1), name="kmean")(kva2d)


def _cumsum_kernel(x_ref, o_ref, *, chunk):
    t = x_ref.shape[1]
    u = (_iota((chunk, chunk), 0) <= _iota((chunk, chunk), 1)).astype(F32)

    def body(i, carry):
        st = pl.multiple_of(i * chunk, chunk)
        seg = x_ref[:, pl.ds(st, chunk)]
        cs = jnp.dot(seg, u, preferred_element_type=F32, precision=lax.Precision.HIGHEST) + carry
        o_ref[:, pl.ds(st, chunk)] = cs
        return cs[:, chunk - 1:chunk]

    lax.fori_loop(0, t // chunk, body, jnp.zeros((x_ref.shape[0], 1), F32))


def _cumsum_t(xt):
    b, h, t = xt.shape
    chunk = 256 if t % 256 == 0 else t
    spec = pl.BlockSpec((None, h, t), lambda i: (i, 0, 0))
    return pl.pallas_call(functools.partial(_cumsum_kernel, chunk=chunk), grid=(b,), in_specs=[spec], out_specs=spec,
                          out_shape=jax.ShapeDtypeStruct(xt.shape, F32), compiler_params=_params(1),
                          name="cumsum")(xt)


def _tri_tables(nq):
    qs, ks = [], []
    for qi in range(nq):
        for kj in range(qi + 1):
            qs.append(qi)
            ks.append(kj)
    return jnp.asarray(np.asarray(qs, np.int32)), jnp.asarray(np.asarray(ks, np.int32))


def _flash_update(s, vext, m_ref, acc_ref, idx):
    m_prev = m_ref[idx]
    m_next = jnp.maximum(m_prev, jnp.max(s, axis=1, keepdims=True))
    alpha = jnp.exp2(m_prev - m_next)
    p = jnp.exp2(s - jnp.tile(m_next, (1, s.shape[1] // LANES)))
    pv = jnp.dot(p.astype(BF16), vext, preferred_element_type=F32)
    acc_ref[idx] = jnp.tile(alpha, (1, pv.shape[1] // LANES)) * acc_ref[idx] + pv
    m_ref[idx] = m_next


def _init_flash(m_ref, acc_ref):
    m_ref[...] = jnp.full(m_ref.shape, NEG, F32)
    acc_ref[...] = jnp.zeros(acc_ref.shape, F32)


def _causal(tile):
    return _iota((tile, tile), 1) <= _iota((tile, tile), 0)


def _pad_heads(q_ref, qs_ref, n):
    zeros = jnp.zeros((q_ref.shape[0], HEAD_DIM), BF16)
    for i in range(n):
        qs_ref[i] = jnp.concatenate([q_ref[:, i * HEAD_DIM:(i + 1) * HEAD_DIM], zeros], axis=1)


def _merge_heads(acc_ref):
    outs = []
    for g in range(GROUP):
        a = acc_ref[g]
        outs.append(a / pltpu.roll(a, HEAD_DIM, axis=1))
    low = _iota((1, LANES), 1) < HEAD_DIM
    return jnp.where(low, outs[0], pltpu.roll(outs[1], HEAD_DIM, axis=1))


def _fox_kernel(qi_tab, kj_tab, q_ref, kv_ref, cq_ref, ck_ref, o_ref, qs_ref, m_ref, acc_ref, *, tile):
    step = pl.program_id(2)
    qi, kj = qi_tab[step], kj_tab[step]

    @pl.when(kj == 0)
    def _():
        _init_flash(m_ref, acc_ref)
        _pad_heads(q_ref, qs_ref, GROUP)

    def update(masked):
        k = kv_ref[:, :LANES]
        vext = kv_ref[:, LANES:]
        for g in range(GROUP):
            s = _dot_t(qs_ref[g], k) + (cq_ref[g][:, 0:1] - ck_ref[g]) * LOG2E
            if masked:
                s = jnp.where(_causal(tile), s, NEG)
            _flash_update(s, vext, m_ref, acc_ref, g)

    @pl.when(kj < qi)
    def _():
        update(False)

    @pl.when(kj == qi)
    def _():
        update(True)
        o_ref[...] = _merge_heads(acc_ref).astype(o_ref.dtype)


def _fox_prompt(q_all, kvx, c_t, *, tile):
    b, t, _ = q_all.shape
    qi_tab, kj_tab = _tri_tables(t // tile)
    xb = OFF_XB // KVX_AB
    grid_spec = pltpu.PrefetchScalarGridSpec(
        num_scalar_prefetch=2, grid=(b, KV_B, int(qi_tab.shape[0])),
        in_specs=[
            pl.BlockSpec((None, tile, LANES), lambda bi, h, s, qt, kt: (bi, qt[s], KV_A + h)),
            pl.BlockSpec((None, tile, KVX_AB), lambda bi, h, s, qt, kt: (bi, kt[s], xb + h)),
            pl.BlockSpec((None, GROUP, 1, tile), lambda bi, h, s, qt, kt: (bi, h, 0, qt[s])),
            pl.BlockSpec((None, GROUP, 1, tile), lambda bi, h, s, qt, kt: (bi, h, 0, kt[s])),
        ],
        out_specs=pl.BlockSpec((None, tile, LANES), lambda bi, h, s, qt, kt: (bi, qt[s], h)),
        scratch_shapes=[pltpu.VMEM((GROUP, tile, LANES), BF16), pltpu.VMEM((GROUP, tile, LANES), F32),
                        pltpu.VMEM((GROUP, tile, LANES), F32)])
    return pl.pallas_call(
        functools.partial(_fox_kernel, tile=tile), grid_spec=grid_spec,
        out_shape=jax.ShapeDtypeStruct((b, t, W_BRANCH), BF16), compiler_params=_params(3),
        name="fox_prompt")(qi_tab, kj_tab, q_all, kvx, c_t[:, :, None, :], c_t[:, :, None, :])


def _diff_kernel(qi_tab, kj_tab, slope_ref, q_ref, kv_ref, lam_ref, sg_ref, o_ref, qs_ref, m_ref, acc_ref, *, tile,
                 lam_init):
    hk = pl.program_id(1)
    step = pl.program_id(2)
    qi, kj = qi_tab[step], kj_tab[step]

    @pl.when(kj == 0)
    def _():
        _init_flash(m_ref, acc_ref)
        _pad_heads(q_ref, qs_ref, 2 * GROUP)

    def update(masked):
        vext = kv_ref[:, 2 * LANES:]
        rel = ((kj - qi) * tile + _iota((1, tile), 1)).astype(F32)
        for g in range(GROUP):
            bias = slope_ref[hk * GROUP + g] * rel
            for m in range(2):
                s = _dot_t(qs_ref[m * GROUP + g], kv_ref[:, m * LANES:(m + 1) * LANES]) + bias
                if masked:
                    s = jnp.where(_causal(tile), s, NEG)
                _flash_update(s, vext, m_ref, acc_ref, m * GROUP + g)

    @pl.when(kj < qi)
    def _():
        update(False)

    @pl.when(kj == qi)
    def _():
        update(True)
        lam = lam_ref[...]
        lam_val = (jnp.exp(jnp.sum(lam[0:1] * lam[1:2], axis=1, keepdims=True))
                   - jnp.exp(jnp.sum(lam[2:3] * lam[3:4], axis=1, keepdims=True)) + lam_init)
        for g in range(GROUP):
            a0, a1 = acc_ref[g], acc_ref[GROUP + g]
            o = a0[:, :V_DIM_C] / a0[:, V_DIM_C:] - lam_val * (a1[:, :V_DIM_C] / a1[:, V_DIM_C:])
            ms = jnp.mean(o * o, axis=-1, keepdims=True)
            o = (o * lax.rsqrt(ms + RMS_EPS)) * sg_ref[...] * (1.0 - lam_init)
            o_ref[:, g * V_DIM_C:(g + 1) * V_DIM_C] = o.astype(o_ref.dtype)


def _diff_prompt(q_all, kvx, slopes, lam_l, sg_l, *, tile, lam_init):
    b, t, _ = q_all.shape
    qi_tab, kj_tab = _tri_tables(t // tile)
    xc = OFF_XC // KVX_C
    grid_spec = pltpu.PrefetchScalarGridSpec(
        num_scalar_prefetch=2, grid=(b, KV_C, int(qi_tab.shape[0])),
        in_specs=[
            pl.BlockSpec(memory_space=pltpu.SMEM),
            pl.BlockSpec((None, tile, 2 * LANES), lambda bi, h, s, qt, kt: (bi, qt[s], 4 + h)),
            pl.BlockSpec((None, tile, KVX_C), lambda bi, h, s, qt, kt: (bi, kt[s], xc + h)),
            pl.BlockSpec((4, HEAD_DIM), lambda bi, h, s, qt, kt: (0, 0)),
            pl.BlockSpec((1, V_DIM_C), lambda bi, h, s, qt, kt: (0, 0)),
        ],
        out_specs=pl.BlockSpec((None, tile, 2 * LANES), lambda bi, h, s, qt, kt: (bi, qt[s], h)),
        scratch_shapes=[pltpu.VMEM((2 * GROUP, tile, LANES), BF16), pltpu.VMEM((2 * GROUP, tile, LANES), F32),
                        pltpu.VMEM((2 * GROUP, tile, 2 * V_DIM_C), F32)])
    return pl.pallas_call(
        functools.partial(_diff_kernel, tile=tile, lam_init=lam_init), grid_spec=grid_spec,
        out_shape=jax.ShapeDtypeStruct((b, t, W_BRANCH), BF16), compiler_params=_params(3),
        name="diff_prompt")(qi_tab, kj_tab, slopes, q_all, kvx, lam_l, sg_l)


def _topk_keep(gate, blk, own, past, nblk, axis):
    gm = jnp.where(past, gate, -jnp.inf)
    cnt = jnp.zeros(gate.shape, jnp.int32)
    for jp in range(nblk):
        gj = gm[jp:jp + 1, :] if axis == 0 else gm[:, jp:jp + 1]
        ahead = (gj > gm) | ((gj == gm) & (jp < blk))
        cnt = cnt + ahead.astype(jnp.int32)
    return (past & (cnt < MOBA_TOPK)) | (blk == own)


def _moba_kernel(qi_tab, kj_tab, slope_ref, q_ref, kv_ref, km_ref, o_ref, qa_ref, m_ref, acc_ref, *, tile, nblk):
    hk = pl.program_id(1)
    step = pl.program_id(2)
    qi, kj = qi_tab[step], kj_tab[step]
    bpt = tile // MOBA_BLOCK
    nbp = -(-nblk // SUBLANES) * SUBLANES

    @pl.when(kj == 0)
    def _():
        _init_flash(m_ref, acc_ref)
        own = lax.shift_right_logical(qi * tile + _iota((1, tile), 1), _log2(MOBA_BLOCK))
        blk = _iota((nbp, 1), 0)
        zeros = jnp.zeros((tile, HEAD_DIM), BF16)
        for g in range(GROUP):
            qg = q_ref[:, g * HEAD_DIM:(g + 1) * HEAD_DIM]
            gate = _dot_t(km_ref[:nbp, :HEAD_DIM], qg.astype(F32), precision=lax.Precision.HIGHEST)
            keep = _topk_keep(gate, blk, own, blk < own, nblk, 0)
            bias = jnp.where(keep, 0.0, NEG)
            bias = jnp.concatenate([bias, jnp.zeros((LANES - nbp, tile), F32)], axis=0)
            qa_ref[g, :, :LANES] = jnp.concatenate([qg, zeros], axis=1)
            qa_ref[g, :, LANES:] = bias.T.astype(BF16)

    def update(masked):
        key_blk = kj * bpt + lax.shift_right_logical(_iota((tile, 1), 0), _log2(MOBA_BLOCK))
        onehot = jnp.where(_iota((1, LANES), 1) == key_blk, 1.0, 0.0).astype(BF16)
        kaug = jnp.concatenate([kv_ref[:, :LANES], onehot], axis=1)
        vext = kv_ref[:, LANES:]
        rel = ((kj - qi) * tile + _iota((1, tile), 1)).astype(F32)
        for g in range(GROUP):
            s = _dot_t(qa_ref[g], kaug) + slope_ref[hk * GROUP + g] * rel
            if masked:
                s = jnp.where(_causal(tile), s, NEG)
            _flash_update(s, vext, m_ref, acc_ref, g)

    @pl.when(kj < qi)
    def _():
        update(False)

    @pl.when(kj == qi)
    def _():
        update(True)
        o_ref[...] = _merge_heads(acc_ref).astype(o_ref.dtype)


def _moba_prompt(q_all, kvx, kmean_pad, slopes, *, tile):
    b, t, _ = q_all.shape
    qi_tab, kj_tab = _tri_tables(t // tile)
    grid_spec = pltpu.PrefetchScalarGridSpec(
        num_scalar_prefetch=2, grid=(b, KV_A, int(qi_tab.shape[0])),
        in_specs=[
            pl.BlockSpec(memory_space=pltpu.SMEM),
            pl.BlockSpec((None, tile, LANES), lambda bi, h, s, qt, kt: (bi, qt[s], h)),
            pl.BlockSpec((None, tile, KVX_AB), lambda bi, h, s, qt, kt: (bi, kt[s], h)),
            pl.BlockSpec((None, LANES, LANES), lambda bi, h, s, qt, kt: (bi, 0, h)),
        ],
        out_specs=pl.BlockSpec((None, tile, LANES), lambda bi, h, s, qt, kt: (bi, qt[s], h)),
        scratch_shapes=[pltpu.VMEM((GROUP, tile, 2 * LANES), BF16), pltpu.VMEM((GROUP, tile, LANES), F32),
                        pltpu.VMEM((GROUP, tile, LANES), F32)])
    return pl.pallas_call(
        functools.partial(_moba_kernel, tile=tile, nblk=t // MOBA_BLOCK), grid_spec=grid_spec,
        out_shape=jax.ShapeDtypeStruct((b, t, W_BRANCH), BF16), compiler_params=_params(3),
        name="moba_prompt")(qi_tab, kj_tab, slopes, q_all, kvx, kmean_pad)


def _rows_of(q_ref, cols):
    return jnp.concatenate([q_ref[:, c:c + HEAD_DIM] for c in cols], axis=0)


def _col_per_group(vals, n_tok):
    return jnp.concatenate([jnp.full((n_tok, 1), v, F32) for v in vals], axis=0)


def _rows_per_group(x, n_tok):
    return jnp.concatenate([jnp.broadcast_to(x[i:i + 1], (n_tok, x.shape[1])) for i in range(x.shape[0])], axis=0)


def _head_match(rows, cols, nkv, n_tok, col_head):
    row_head = lax.shift_right_logical(_iota((rows, 1), 0), _log2(GROUP * n_tok)) & (nkv - 1)
    return jnp.where(col_head == row_head, 0.0, NEG)


def _new_rows(kvn_ref, nkv):
    w = kvn_ref.shape[1] // nkv
    return jnp.concatenate([kvn_ref[:, h * w:(h + 1) * w] for h in range(nkv)], axis=0)


def _new_bias(rows, nkv, n_tok):
    col = _iota((1, nkv * n_tok), 1)
    bias = _head_match(rows, nkv * n_tok, nkv, n_tok, lax.shift_right_logical(col, _log2(n_tok)))
    visible = (col & (n_tok - 1)) <= (_iota((rows, 1), 0) & (n_tok - 1))
    return jnp.where(visible, bias, NEG), (col & (n_tok - 1)).astype(F32)


def _multi_update(s_list, v_list, m_ref, l_ref, acc_ref, shift=0):
    idx = ...
    m_prev = m_ref[idx]
    smax = s_list[0]
    for s in s_list[1:]:
        smax = jnp.maximum(smax, s)
    m_next = jnp.maximum(m_prev, jnp.max(smax, axis=1, keepdims=True))
    alpha = jnp.exp2(m_prev - m_next)
    psum, pv = None, None
    for s, v in zip(s_list, v_list):
        p = jnp.exp2(s - m_next)
        pm = pltpu.roll(p, shift, axis=1) if shift else p
        d = jnp.dot(pm.astype(BF16), v, preferred_element_type=F32)
        psum = p if psum is None else psum + p
        pv = d if pv is None else pv + d
    l_ref[idx] = alpha * l_ref[idx] + jnp.sum(psum, axis=1, keepdims=True)
    acc_ref[idx] = alpha * acc_ref[idx] + pv
    m_ref[idx] = m_next


def _init_multi(m_ref, l_ref, acc_ref):
    m_ref[...] = jnp.full(m_ref.shape, NEG, F32)
    l_ref[...] = jnp.zeros(l_ref.shape, F32)
    acc_ref[...] = jnp.zeros(acc_ref.shape, F32)


def _fox_sample_kernel(pt_ref, q_ref, kvn_ref, lfn_ref, *rest, npg, n_tok):
    pages, lfs = rest[:npg], rest[npg:2 * npg]
    o_ref, m_ref, l_ref, acc_ref, carry_ref = rest[2 * npg:]
    j = pl.program_id(1)
    rows, pr = H_B * n_tok, PAGE_SIZE * KV_B

    @pl.when(j == 0)
    def _():
        _init_multi(m_ref, l_ref, acc_ref)
        carry_ref[...] = jnp.zeros(carry_ref.shape, F32)

    q = _rows_of(q_ref, [h * HEAD_DIM for h in range(H_B)]).astype(BF16)
    col = _iota((1, pr), 1)
    match = _head_match(rows, pr, KV_B, n_tok, col & (KV_B - 1))
    upper = (_iota((PAGE_SIZE, 1), 0) <= lax.shift_right_logical(col, _log2(KV_B))).astype(F32)
    carry = carry_ref[...]
    s_list, v_list = [], []
    c_local = _dot_tl(jnp.concatenate([lf[...] for lf in lfs], axis=1), upper)
    for r in range(npg):
        c_full = c_local[r * H_B:(r + 1) * H_B] + carry
        carry = c_full[:, pr - 1:pr]
        bias = match - _rows_per_group(c_full, n_tok) * LOG2E
        s_list.append(_dot_t(q, pages[r][:, :HEAD_DIM].astype(BF16)) + bias)
        v_list.append(pages[r][:, HEAD_DIM:].astype(BF16))
    carry_ref[...] = carry
    _multi_update(s_list, v_list, m_ref, l_ref, acc_ref)

    @pl.when(j == pl.num_programs(1) - 1)
    def _():
        kvn = _new_rows(kvn_ref, KV_B)
        bias, _ = _new_bias(rows, KV_B, n_tok)
        tok = _iota((1, KV_B * n_tok), 1) & (n_tok - 1)
        upper_n = (_iota((n_tok, 1), 0) <= tok).astype(F32)
        c_new = _dot_tl(lfn_ref[...], upper_n) + carry
        s = _dot_t(q, kvn[:, :HEAD_DIM].astype(BF16)) + (bias - _rows_per_group(c_new, n_tok) * LOG2E)
        _multi_update([s], [kvn[:, HEAD_DIM:].astype(BF16)], m_ref, l_ref, acc_ref)
        o = acc_ref[...] / l_ref[...]
        for h in range(H_B):
            o_ref[:, h * HEAD_DIM:(h + 1) * HEAD_DIM] = o[h * n_tok:(h + 1) * n_tok]


def _diff_sample_kernel(pt_ref, slope_ref, lam_ref, sg_ref, q_ref, kvn_ref, *rest, npg, n_tok, lam_init):
    pages = rest[:npg]
    o_ref, m_ref, l_ref, acc_ref = rest[npg:]
    j = pl.program_id(1)
    half = H_C * n_tok
    rows, pr = 2 * half, 2 * PAGE_SIZE * KV_C
    n_past = pl.num_programs(1) * npg * PAGE_SIZE

    @pl.when(j == 0)
    def _():
        _init_multi(m_ref, l_ref, acc_ref)

    qs = [_rows_of(q_ref, [((hk * 2 + m) * GROUP + g) * HEAD_DIM for hk in range(KV_C) for g in range(GROUP)])
          .astype(BF16) for m in range(2)]
    slope = _col_per_group([slope_ref[h] for h in range(H_C)] * 2, n_tok)

    def logits(kv, bias):
        return jnp.concatenate([_dot_t(qs[m], kv[:, m * HEAD_DIM:(m + 1) * HEAD_DIM].astype(BF16))
                                for m in range(2)], axis=0) + bias

    col = _iota((1, pr), 1)
    key_head = jnp.where((col & KV_C) == 0, col & (KV_C - 1), -1)
    match = _head_match(rows, pr, KV_C, n_tok, key_head)
    tok = lax.shift_right_logical(col, _log2(2 * KV_C))
    s_list, v_list = [], []
    for r in range(npg):
        rel = ((j * npg + r) * PAGE_SIZE - n_past + tok).astype(F32)
        s_list.append(logits(pages[r], match + slope * rel))
        v_list.append(pages[r][...].astype(BF16))
    _multi_update(s_list, v_list, m_ref, l_ref, acc_ref, shift=KV_C)

    @pl.when(j == pl.num_programs(1) - 1)
    def _():
        kvn = _new_rows(kvn_ref, KV_C)
        bias, rel = _new_bias(rows, KV_C, n_tok)
        _multi_update([logits(kvn, bias + slope * rel)], [kvn[:, 2 * HEAD_DIM:].astype(BF16)], m_ref, l_ref, acc_ref)
        lam = lam_ref[...]
        lam_val = (jnp.exp(jnp.sum(lam[0:1] * lam[1:2], axis=1, keepdims=True))
                   - jnp.exp(jnp.sum(lam[2:3] * lam[3:4], axis=1, keepdims=True)) + lam_init)
        on = acc_ref[...] / l_ref[...]
        o = on[:half] - lam_val * on[half:]
        ms = jnp.mean(o * o, axis=-1, keepdims=True)
        o = (o * lax.rsqrt(ms + RMS_EPS)) * sg_ref[...] * (1.0 - lam_init)
        for h in range(H_C):
            o_ref[:, h * V_DIM_C:(h + 1) * V_DIM_C] = o[h * n_tok:(h + 1) * n_tok]


def _moba_sample_kernel(pt_ref, slope_ref, q_ref, kvn_ref, *rest, npg, n_tok, nblk):
    pages = rest[:npg]
    o_ref, mb_ref, lb_ref, gb_ref, ob_ref = rest[npg:]
    j = pl.program_id(1)
    rows, pr = H_A * n_tok, PAGE_SIZE * KV_A
    ppb = MOBA_BLOCK // PAGE_SIZE
    bps = npg // ppb
    lane = _iota((1, LANES), 1)
    n_past = pl.num_programs(1) * npg * PAGE_SIZE

    @pl.when(j == 0)
    def _():
        mb_ref[...] = jnp.full(mb_ref.shape, NEG, F32)
        lb_ref[...] = jnp.zeros(lb_ref.shape, F32)
        gb_ref[...] = jnp.zeros(gb_ref.shape, F32)

    qf = _rows_of(q_ref, [h * HEAD_DIM for h in range(H_A)])
    q = qf.astype(BF16)
    slope = _col_per_group([slope_ref[h] for h in range(H_A)], n_tok)
    col = _iota((1, pr), 1)
    match = _head_match(rows, pr, KV_A, n_tok, col & (KV_A - 1))
    tok = lax.shift_right_logical(col, _log2(KV_A))
    row_kv = lax.shift_right_logical(_iota((rows, 1), 0), _log2(GROUP * n_tok))

    for bl in range(bps):
        blk = j * bps + bl
        s_l, v_l, kmean = [], [], None
        for r in range(ppb):
            page = pages[bl * ppb + r]
            rel = ((blk * ppb + r) * PAGE_SIZE - n_past + tok).astype(F32)
            s_l.append(_dot_t(q, page[:, :HEAD_DIM].astype(BF16)) + (match + slope * rel))
            v_l.append(page[:, HEAD_DIM:].astype(BF16))
            km = jnp.sum(page[:, :HEAD_DIM].reshape(pr // SUBLANES, SUBLANES, HEAD_DIM), axis=0)
            kmean = km if kmean is None else kmean + km
        mb = jnp.max(functools.reduce(jnp.maximum, s_l), axis=1, keepdims=True)
        ps = [jnp.exp2(s - mb) for s in s_l]
        lb = jnp.sum(functools.reduce(jnp.add, ps), axis=1, keepdims=True)
        ob_ref[blk] = functools.reduce(
            jnp.add, [jnp.dot(p.astype(BF16), v, preferred_element_type=F32) for p, v in zip(ps, v_l)])
        kmean = (kmean[:KV_A] + kmean[KV_A:]) * (1.0 / MOBA_BLOCK)
        kmean = jnp.concatenate([kmean, jnp.zeros((SUBLANES - KV_A, HEAD_DIM), F32)], axis=0)
        gate_all = _dot_t(qf, kmean, precision=lax.Precision.HIGHEST)
        gate = jnp.sum(jnp.where(_iota((1, SUBLANES), 1) == row_kv, gate_all, 0.0), axis=1, keepdims=True)
        here = lane == blk
        mb_ref[...] = jnp.where(here, mb, mb_ref[...])
        lb_ref[...] = jnp.where(here, lb, lb_ref[...])
        gb_ref[...] = jnp.where(here, gate, gb_ref[...])

    @pl.when(j == pl.num_programs(1) - 1)
    def _():
        keep = _topk_keep(gb_ref[...], lane, nblk, lane < nblk, nblk, 1) & (lane < nblk)
        m_sel = jnp.where(keep, mb_ref[...], NEG)
        kvn = _new_rows(kvn_ref, KV_A)
        bias, rel = _new_bias(rows, KV_A, n_tok)
        s = _dot_t(q, kvn[:, :HEAD_DIM].astype(BF16)) + (bias + slope * rel)
        m_tot = jnp.maximum(jnp.max(s, axis=1, keepdims=True), jnp.max(m_sel, axis=1, keepdims=True))
        p = jnp.exp2(s - m_tot)
        w = jnp.where(keep, jnp.exp2(m_sel - m_tot), 0.0)
        den = jnp.sum(p, axis=1, keepdims=True) + jnp.sum(w * lb_ref[...], axis=1, keepdims=True)
        num = jnp.dot(p.astype(BF16), kvn[:, HEAD_DIM:].astype(BF16), preferred_element_type=F32)
        for b in range(nblk):
            num = num + w[:, b:b + 1] * ob_ref[b]
        o = num / den
        for h in range(H_A):
            o_ref[:, h * HEAD_DIM:(h + 1) * HEAD_DIM] = o[h * n_tok:(h + 1) * n_tok]


def _page_specs(layer, n_pages, npg, page_shape):
    zeros = (0,) * len(page_shape)

    def spec(r):
        return pl.BlockSpec((None, None) + tuple(page_shape),
                            lambda b, j, pt: (layer, pt[b * n_pages + j * npg + r]) + zeros)
    return [spec(r) for r in range(npg)]


def _sample_attention(kind, layer, pt_flat, q_s, kvn, cache, *, n_pages, extra=(), extra_specs=(), logf_cache=None,
                      lfn=None, lam_init=0.0):
    db, n_tok, _ = q_s.shape
    npg = PAGES_PER_STEP if n_pages % PAGES_PER_STEP == 0 else MOBA_BLOCK // PAGE_SIZE
    per_b = lambda w: pl.BlockSpec((None, n_tok, w), lambda b, j, pt: (b, 0, 0))
    qcol = {"moba": 0, "fox": 1, "diff": 2}[kind]
    q_spec = pl.BlockSpec((None, n_tok, W_BRANCH), lambda b, j, pt: (b, 0, qcol))
    in_specs = list(extra_specs) + [q_spec, per_b(W_BRANCH)]
    args = list(extra) + [q_s, kvn]
    if kind == "fox":
        in_specs.append(per_b(H_B))
        args.append(lfn)
    in_specs += _page_specs(layer, n_pages, npg, cache.shape[2:])
    args += [cache] * npg
    col = lambda r, w: pltpu.VMEM((r, w), F32)
    if kind == "fox":
        rows = H_B * n_tok
        in_specs += _page_specs(layer, n_pages, npg, (PAGE_SIZE, H_B))
        args += [logf_cache] * npg
        kern = functools.partial(_fox_sample_kernel, npg=npg, n_tok=n_tok)
        scratch = [col(rows, 1), col(rows, 1), col(rows, HEAD_DIM), col(H_B, 1)]
    elif kind == "diff":
        rows = 2 * H_C * n_tok
        kern = functools.partial(_diff_sample_kernel, npg=npg, n_tok=n_tok, lam_init=lam_init)
        scratch = [col(rows, 1), col(rows, 1), col(rows, V_DIM_C)]
    else:
        rows = H_A * n_tok
        nblk = n_pages * PAGE_SIZE // MOBA_BLOCK
        kern = functools.partial(_moba_sample_kernel, npg=npg, n_tok=n_tok, nblk=nblk)
        scratch = [col(rows, LANES), col(rows, LANES), col(rows, LANES), pltpu.VMEM((nblk, rows, HEAD_DIM), F32)]
    grid_spec = pltpu.PrefetchScalarGridSpec(
        num_scalar_prefetch=1, grid=(db, n_pages // npg), in_specs=in_specs, out_specs=per_b(W_BRANCH),
        scratch_shapes=scratch)
    return pl.pallas_call(kern, grid_spec=grid_spec, out_shape=jax.ShapeDtypeStruct((db, n_tok, W_BRANCH), F32),
                          compiler_params=_params(2), name=kind + "_sample")(pt_flat, *args)


def _alibi_slopes(n):
    return 2.0 ** (-8.0 * jnp.arange(1, n + 1, dtype=F32) / n)


def kernel(x_prompt, x_sample, cache_a_kv, cache_b_kv, cache_b_logf, cache_c_kv, page_table, w_in, w_o_a, w_o_b,
           w_o_c, w_out, norm_g, forget_b, diff_lambda, diff_subln_g, final_norm_g):
    depth = w_in.shape[0]
    bp, t, _ = x_prompt.shape
    db, n_tok, _ = x_sample.shape
    n_pages = page_table.shape[1]
    n_pool = cache_a_kv.shape[1]
    assert t % MOBA_BLOCK == 0 and (n_pages * PAGE_SIZE) % MOBA_BLOCK == 0
    assert t // MOBA_BLOCK <= LANES and n_pages * PAGE_SIZE // MOBA_BLOCK <= LANES
    assert n_tok == SUBLANES

    w_pad = jnp.pad(w_in, ((0, 0), (0, 0), (0, 1)))
    w_perm = (jnp.take(w_pad, jnp.asarray(_PERM), axis=2) * jnp.asarray(_SCALE)).astype(BF16)
    woa, wob, woc, wout = (w.astype(BF16) for w in (w_o_a, w_o_b, w_o_c, w_out))
    norm_g3 = norm_g.reshape(depth, 1, D_MODEL)
    fb3 = jnp.pad(forget_b, ((0, 0), (0, FB_PAD - H_B))).reshape(depth, 1, FB_PAD)
    fg2 = final_norm_g.reshape(1, D_MODEL)
    slopes_a, slopes_c = _alibi_slopes(H_A) * LOG2E, _alibi_slopes(H_C) * LOG2E
    pt_flat = page_table.reshape(-1).astype(jnp.int32)
    ca = cache_a_kv.reshape(depth, n_pool, PAGE_SIZE * KV_A, 2 * HEAD_DIM)
    cb = cache_b_kv.reshape(depth, n_pool, PAGE_SIZE * KV_B, 2 * HEAD_DIM)
    cc = cache_c_kv.reshape(depth, n_pool, PAGE_SIZE, KV_C, 2, LANES).transpose(0, 1, 2, 4, 3, 5)
    cc = cc.reshape(depth, n_pool, 2 * PAGE_SIZE * KV_C, LANES)

    tm_p = 256 if (bp * t) % 256 == 0 else bp * t
    tile = 512 if t % 512 == 0 else t
    ns = db * n_tok
    smem = pl.BlockSpec(memory_space=pltpu.SMEM)

    xp = x_prompt.reshape(bp * t, D_MODEL)
    xs = x_sample.reshape(ns, D_MODEL)
    outs_p = [[], [], [], []]
    outs_s = [[], [], [], []]
    for l in range(depth):
        lam_init = 0.8 - 0.6 * math.exp(-0.3 * l)
        sg = diff_subln_g[l].reshape(1, V_DIM_C)
        final = l == depth - 1

        kva, kvb, kvc, logf, kvx, q_all, gates, gm = _inproj(xp, l, norm_g3, fb3, w_perm, tm=tm_p, q_dtype=BF16)
        for acc, v in zip(outs_p, (kva, kvb, logf, kvc)):
            acc.append(v)
        q3 = q_all.reshape(bp, t, 3 * W_BRANCH)
        kv3 = kvx.reshape(bp, t, W_KVX)
        km = _kmean(kva).reshape(bp, t // MOBA_BLOCK, W_BRANCH)
        km = jnp.pad(km, ((0, 0), (0, LANES - t // MOBA_BLOCK), (0, 0)))
        c_t = _cumsum_t(jnp.swapaxes(logf.reshape(bp, t, H_B), 1, 2))
        oa = _moba_prompt(q3, kv3, km, slopes_a, tile=tile)
        ob = _fox_prompt(q3, kv3, c_t, tile=tile)
        oc = _diff_prompt(q3, kv3, slopes_c, diff_lambda[l], sg, tile=tile, lam_init=lam_init)
        xp = _outproj(xp, oa.reshape(bp * t, W_BRANCH), ob.reshape(bp * t, W_BRANCH), oc.reshape(bp * t, W_BRANCH),
                      gates, gm, l, woa, wob, woc, wout, fg2, tm=tm_p, final=final)

        kva, kvb, kvc, logf, _, q_all, gates, gm = _inproj(xs, l, norm_g3, fb3, w_perm, tm=ns, q_dtype=F32)
        for acc, v in zip(outs_s, (kva, kvb, logf, kvc)):
            acc.append(v)
        q3 = q_all.reshape(db, n_tok, 3 * W_BRANCH)
        r3 = lambda a: a.reshape(db, n_tok, a.shape[-1])
        oa = _sample_attention("moba", l, pt_flat, q3, r3(kva), ca, n_pages=n_pages, extra=(slopes_a,),
                               extra_specs=(smem,))
        ob = _sample_attention("fox", l, pt_flat, q3, r3(kvb), cb, n_pages=n_pages, logf_cache=cache_b_logf,
                               lfn=r3(logf))
        oc = _sample_attention("diff", l, pt_flat, q3, r3(kvc), cc, n_pages=n_pages,
                               extra=(slopes_c, diff_lambda[l], sg),
                               extra_specs=(smem, pl.BlockSpec((4, HEAD_DIM), lambda b, j, pt: (0, 0)),
                                            pl.BlockSpec((1, V_DIM_C), lambda b, j, pt: (0, 0))),
                               lam_init=lam_init)
        xs = _outproj(xs, oa.reshape(ns, W_BRANCH), ob.reshape(ns, W_BRANCH), oc.reshape(ns, W_BRANCH), gates, gm, l,
                      woa, wob, woc, wout, fg2, tm=ns, final=final)

    def stack(vals, b_, t_, tail):
        return jnp.stack(vals).reshape((depth, b_, t_) + tail)

    pa, pb, pf, pc = outs_p
    sa, sb, sf, sc = outs_s
    kv_tail, c_tail = (KV_A, 2 * HEAD_DIM), (KV_C, 2 * HEAD_DIM + V_DIM_C)
    return (xp.reshape(bp, t, D_MODEL), xs.reshape(db, n_tok, D_MODEL),
            stack(pa, bp, t, kv_tail), stack(pb, bp, t, kv_tail), stack(pf, bp, t, (H_B,)), stack(pc, bp, t, c_tail),
            stack(sa, db, n_tok, kv_tail), stack(sb, db, n_tok, kv_tail), stack(sf, db, n_tok, (H_B,)),
            stack(sc, db, n_tok, c_tail))
```

```python
import functools
import math

import jax
import jax.numpy as jnp
import numpy as np
from jax import lax
from jax.experimental import pallas as pl
from jax.experimental.pallas import tpu as pltpu

F32 = jnp.float32
BF16 = jnp.bfloat16

D_MODEL = 1024
HEAD_DIM = 64
PAGE_SIZE = 128
H_A, KV_A = 8, 4
H_B, KV_B = 8, 4
H_C, KV_C = 4, 2
GROUP = 2
V_DIM_C = 2 * HEAD_DIM
MOBA_BLOCK = 256
MOBA_TOPK = 3
N_BRANCH = 3
RMS_EPS = 1e-6
W_BRANCH = 512
IN_NAMES = ("qa", "ka", "va", "ga", "qb", "kb", "vb", "fb", "gb", "qc", "kc", "vc", "gc", "gm")
IN_SIZES = (512, 256, 256, 512, 512, 256, 256, 8, 512, 512, 256, 256, 512, N_BRANCH * D_MODEL)
D_IN = sum(IN_SIZES)

NEG = -1e30
LOG2E = math.log2(math.e)
LANES = 128
SUBLANES = 8
FB_PAD = LANES
OFF_KVA, OFF_KVB, OFF_KVC, OFF_FB = 0, 512, 1024, 1536
OFF_Q = OFF_FB + FB_PAD
OFF_G = OFF_Q + 3 * W_BRANCH
OFF_GM = OFF_G + 3 * W_BRANCH
D_PERM = OFF_GM + N_BRANCH * D_MODEL
KVX_AB = 4 * HEAD_DIM
KVX_C = 8 * HEAD_DIM
OFF_XA, OFF_XB, OFF_XC = 0, KV_A * KVX_AB, KV_A * KVX_AB + KV_B * KVX_AB
W_KVX = OFF_XC + KV_C * KVX_C
VMEM_LIMIT = 56 * 1024 * 1024
PAGES_PER_STEP = 32


def _perm_and_scale():
    off = dict(zip(IN_NAMES, np.concatenate([[0], np.cumsum(IN_SIZES)[:-1]])))
    d = HEAD_DIM
    cols = []
    for kn, vn, nkv in (("ka", "va", KV_A), ("kb", "vb", KV_B)):
        for h in range(nkv):
            cols += list(range(off[kn] + h * d, off[kn] + (h + 1) * d))
            cols += list(range(off[vn] + h * d, off[vn] + (h + 1) * d))
    for h in range(KV_C):
        for m in range(2):
            cols += list(range(off["kc"] + (m * KV_C + h) * d, off["kc"] + (m * KV_C + h + 1) * d))
        cols += list(range(off["vc"] + h * V_DIM_C, off["vc"] + (h + 1) * V_DIM_C))
    cols += list(range(off["fb"], off["fb"] + H_B)) + [D_IN] * (FB_PAD - H_B)
    cols += list(range(off["qa"], off["qa"] + 512))
    cols += list(range(off["qb"], off["qb"] + 512))
    for h in range(KV_C):
        for m in range(2):
            for g in range(GROUP):
                s = off["qc"] + ((m * KV_C + h) * GROUP + g) * d
                cols += list(range(s, s + d))
    for n in ("ga", "gb", "gc"):
        cols += list(range(off[n], off[n] + 512))
    cols += list(range(off["gm"], off["gm"] + N_BRANCH * D_MODEL))
    perm = np.asarray(cols, np.int32)
    assert perm.shape[0] == D_PERM
    scale = np.ones((D_PERM,), np.float32)
    scale[OFF_Q:OFF_G] = HEAD_DIM ** -0.5 * LOG2E
    return perm, scale


_PERM, _SCALE = _perm_and_scale()


def _permute_columns(w_in):
    pieces, start = [], 0
    for i in range(1, D_PERM + 1):
        joined = i < D_PERM and _SCALE[i] == _SCALE[start] and (
            _PERM[i] == _PERM[i - 1] + 1 or _PERM[i] == _PERM[i - 1] == D_IN)
        if not joined:
            a, n = int(_PERM[start]), i - start
            if a == D_IN:
                pieces.append(jnp.zeros(w_in.shape[:2] + (n,), BF16))
            else:
                pieces.append((w_in[:, :, a:a + n] * float(_SCALE[start])).astype(BF16))
            start = i
    return jnp.concatenate(pieces, axis=2)


def _params(n_axes):
    return pltpu.CompilerParams(dimension_semantics=("arbitrary",) * n_axes, vmem_limit_bytes=VMEM_LIMIT)


def _sigmoid(x):
    return 1.0 / (1.0 + jnp.exp(-x))


def _dot_t(a, b, precision=None):
    return lax.dot_general(a, b, (((1,), (1,)), ((), ())), preferred_element_type=F32, precision=precision)


def _dot_tl(a, b):
    return lax.dot_general(a, b, (((0,), (0,)), ((), ())), preferred_element_type=F32,
                           precision=lax.Precision.HIGHEST)


def _iota(shape, axis):
    return lax.broadcasted_iota(jnp.int32, shape, axis)


def _log2(n):
    assert n & (n - 1) == 0
    return n.bit_length() - 1


def _inproj_kernel(x_ref, g_ref, fb_ref, w_ref, kva_ref, kvb_ref, kvc_ref, logf_ref, kvx_ref, q_ref, gate_ref,
                   gm_ref):
    x = x_ref[...]
    ms = jnp.mean(x * x, axis=-1, keepdims=True)
    h = ((x * lax.rsqrt(ms + RMS_EPS)) * g_ref[...]).astype(BF16)

    def mm(a, n):
        return jnp.dot(h, w_ref[:, a:a + n], preferred_element_type=F32)

    low = (_iota((1, W_BRANCH), 1) & (LANES - 1)) < HEAD_DIM
    for i, (ref, off_x) in enumerate(((kva_ref, OFF_XA), (kvb_ref, OFF_XB))):
        z = mm(i * W_BRANCH, W_BRANCH)
        ref[...] = z
        kx = jnp.where(low, z, 0.0).astype(BF16)
        vx = jnp.where(low, pltpu.roll(z, W_BRANCH - HEAD_DIM, axis=1), 1.0).astype(BF16)
        for hk in range(KV_A):
            kvx_ref[:, off_x + hk * KVX_AB: off_x + hk * KVX_AB + LANES] = kx[:, hk * LANES:(hk + 1) * LANES]
            kvx_ref[:, off_x + hk * KVX_AB + LANES: off_x + (hk + 1) * KVX_AB] = vx[:, hk * LANES:(hk + 1) * LANES]
    z = mm(OFF_KVC, W_BRANCH)
    kvc_ref[...] = z
    k1 = jnp.where(low, z, 0.0).astype(BF16)
    k2 = jnp.where(low, pltpu.roll(z, W_BRANCH - HEAD_DIM, axis=1), 0.0).astype(BF16)
    for hk in range(KV_C):
        c, o = hk * 2 * LANES, OFF_XC + hk * KVX_C
        kvx_ref[:, o: o + LANES] = k1[:, c:c + LANES]
        kvx_ref[:, o + LANES: o + 2 * LANES] = k2[:, c:c + LANES]
        kvx_ref[:, o + 2 * LANES: o + 3 * LANES] = z[:, c + LANES: c + 2 * LANES].astype(BF16)
        kvx_ref[:, o + 3 * LANES: o + 4 * LANES] = jnp.ones((z.shape[0], LANES), BF16)
    zf = mm(OFF_FB, FB_PAD) + fb_ref[...]
    lf = jnp.minimum(zf, 0.0) - jnp.log(1.0 + jnp.exp(-jnp.abs(zf)))
    logf_ref[...] = lf[:, :H_B]
    for i in range(3):
        q_ref[:, i * W_BRANCH:(i + 1) * W_BRANCH] = mm(OFF_Q + i * W_BRANCH, W_BRANCH).astype(q_ref.dtype)
        gate_ref[:, i * W_BRANCH:(i + 1) * W_BRANCH] = mm(OFF_G + i * W_BRANCH, W_BRANCH).astype(BF16)
    for i in range(N_BRANCH * D_MODEL // W_BRANCH):
        gm_ref[:, i * W_BRANCH:(i + 1) * W_BRANCH] = mm(OFF_GM + i * W_BRANCH, W_BRANCH).astype(BF16)


def _inproj(x2d, layer, norm_g3, fb3, w_perm, *, tm, q_dtype):
    n = x2d.shape[0]
    row = lambda w: pl.BlockSpec((tm, w), lambda i: (i, 0))
    lay = lambda shp: pl.BlockSpec((None,) + shp, lambda i: (layer, 0, 0))
    out_shape = (
        jax.ShapeDtypeStruct((n, W_BRANCH), F32), jax.ShapeDtypeStruct((n, W_BRANCH), F32),
        jax.ShapeDtypeStruct((n, W_BRANCH), F32), jax.ShapeDtypeStruct((n, H_B), F32),
        jax.ShapeDtypeStruct((n, W_KVX), BF16), jax.ShapeDtypeStruct((n, 3 * W_BRANCH), q_dtype),
        jax.ShapeDtypeStruct((n, 3 * W_BRANCH), BF16), jax.ShapeDtypeStruct((n, N_BRANCH * D_MODEL), BF16))
    return pl.pallas_call(
        _inproj_kernel, grid=(n // tm,),
        in_specs=[row(D_MODEL), lay((1, D_MODEL)), lay((1, FB_PAD)), lay((D_MODEL, D_PERM))],
        out_specs=[row(W_BRANCH), row(W_BRANCH), row(W_BRANCH), row(H_B), row(W_KVX), row(3 * W_BRANCH),
                   row(3 * W_BRANCH), row(N_BRANCH * D_MODEL)],
        out_shape=out_shape, compiler_params=_params(1), name="inproj")(x2d, norm_g3, fb3, w_perm)


def _outproj_kernel(x_ref, oa_ref, ob_ref, oc_ref, gate_ref, gm_ref, woa_ref, wob_ref, woc_ref, wout_ref, fg_ref,
                    y_ref, *, final):
    def branch(i, o_ref, w_ref):
        g = gate_ref[:, i * W_BRANCH:(i + 1) * W_BRANCH].astype(F32)
        u = (o_ref[...].astype(F32) * (g * _sigmoid(g))).astype(BF16)
        p = jnp.dot(u, w_ref[...], preferred_element_type=F32)
        return _sigmoid(gm_ref[:, i * D_MODEL:(i + 1) * D_MODEL].astype(F32)) * p

    merged = branch(0, oa_ref, woa_ref) + branch(1, ob_ref, wob_ref) + branch(2, oc_ref, woc_ref)
    y = x_ref[...] + jnp.dot(merged.astype(BF16), wout_ref[...], preferred_element_type=F32)
    if final:
        ms = jnp.mean(y * y, axis=-1, keepdims=True)
        y = (y * lax.rsqrt(ms + RMS_EPS)) * fg_ref[...]
    y_ref[...] = y


def _outproj(x2d, oa, ob, oc, gates, gm, layer, woa, wob, woc, wout, fg2, *, tm, final):
    n = x2d.shape[0]
    row = lambda w: pl.BlockSpec((tm, w), lambda i: (i, 0))
    lay = lambda shp: pl.BlockSpec((None,) + shp, lambda i: (layer, 0, 0))
    return pl.pallas_call(
        functools.partial(_outproj_kernel, final=final), grid=(n // tm,),
        in_specs=[row(D_MODEL), row(W_BRANCH), row(W_BRANCH), row(W_BRANCH), row(3 * W_BRANCH),
                  row(N_BRANCH * D_MODEL), lay((W_BRANCH, D_MODEL)), lay((W_BRANCH, D_MODEL)),
                  lay((W_BRANCH, D_MODEL)), lay((D_MODEL, D_MODEL)), pl.BlockSpec((1, D_MODEL), lambda i: (0, 0))],
        out_specs=row(D_MODEL), out_shape=jax.ShapeDtypeStruct((n, D_MODEL), F32),
        compiler_params=_params(1), name="outproj")(x2d, oa, ob, oc, gates, gm, woa, wob, woc, wout, fg2)


def _kmean_kernel(kv_ref, o_ref, *, nb):
    for i in range(nb):
        o_ref[i:i + 1, :] = jnp.mean(kv_ref[i * MOBA_BLOCK:(i + 1) * MOBA_BLOCK, :], axis=0, keepdims=True)


def _kmean(kva2d):
    nblk = kva2d.shape[0] // MOBA_BLOCK
    nb = SUBLANES if nblk % SUBLANES == 0 else nblk
    return pl.pallas_call(
        functools.partial(_kmean_kernel, nb=nb), grid=(nblk // nb,),
        in_specs=[pl.BlockSpec((nb * MOBA_BLOCK, W_BRANCH), lambda i: (i, 0))],
        out_specs=pl.BlockSpec((nb, W_BRANCH), lambda i: (i, 0)),
        out_shape=jax.ShapeDtypeStruct((nblk, W_BRANCH), F32), compiler_params=_params(1), name="kmean")(kva2d)


def _cumsum_kernel(x_ref, o_ref, *, chunk):
    t = x_ref.shape[1]
    u = (_iota((chunk, chunk), 0) <= _iota((chunk, chunk), 1)).astype(F32)

    def body(i, carry):
        st = pl.multiple_of(i * chunk, chunk)
        seg = x_ref[:, pl.ds(st, chunk)]
        cs = jnp.dot(seg, u, preferred_element_type=F32, precision=lax.Precision.HIGHEST) + carry
        o_ref[:, pl.ds(st, chunk)] = cs
        return cs[:, chunk - 1:chunk]

    lax.fori_loop(0, t // chunk, body, jnp.zeros((x_ref.shape[0], 1), F32))


def _cumsum_t(xt):
    b, h, t = xt.shape
    chunk = 256 if t % 256 == 0 else t
    spec = pl.BlockSpec((None, h, t), lambda i: (i, 0, 0))
    return pl.pallas_call(functools.partial(_cumsum_kernel, chunk=chunk), grid=(b,), in_specs=[spec], out_specs=spec,
                          out_shape=jax.ShapeDtypeStruct(xt.shape, F32), compiler_params=_params(1),
                          name="cumsum")(xt)


def _tri_tables(nq):
    qs, ks = [], []
    for qi in range(nq):
        for kj in range(qi + 1):
            qs.append(qi)
            ks.append(kj)
    return jnp.asarray(np.asarray(qs, np.int32)), jnp.asarray(np.asarray(ks, np.int32))


def _flash_update(s, vext, m_ref, acc_ref, idx, r0=0):
    rows = slice(r0, r0 + s.shape[0])
    m_prev = m_ref[idx, rows]
    m_next = jnp.maximum(m_prev, jnp.max(s, axis=1, keepdims=True))
    alpha = jnp.exp2(m_prev - m_next)
    p = jnp.exp2(s - jnp.tile(m_next, (1, s.shape[1] // LANES)))
    pv = jnp.dot(p.astype(BF16), vext, preferred_element_type=F32)
    acc_ref[idx, rows] = jnp.tile(alpha, (1, pv.shape[1] // LANES)) * acc_ref[idx, rows] + pv
    m_ref[idx, rows] = m_next


def _init_flash(m_ref, acc_ref):
    m_ref[...] = jnp.full(m_ref.shape, NEG, F32)
    acc_ref[...] = jnp.zeros(acc_ref.shape, F32)


def _tile_update(logits, vext, m_ref, acc_ref, idx, tile, diag):
    if not diag:
        _flash_update(logits(0, tile, 0, tile), vext, m_ref, acc_ref, idx)
        return
    h = tile // 2
    s = logits(0, tile, 0, h)
    _flash_update(jnp.where(_iota(s.shape, 1) <= _iota(s.shape, 0), s, NEG), vext[:h], m_ref, acc_ref, idx)
    s = logits(h, tile, h, tile)
    _flash_update(jnp.where(_iota(s.shape, 1) <= _iota(s.shape, 0), s, NEG), vext[h:], m_ref, acc_ref, idx, r0=h)


def _pad_heads(q_ref, qs_ref, n):
    zeros = jnp.zeros((q_ref.shape[0], HEAD_DIM), BF16)
    for i in range(n):
        qs_ref[i] = jnp.concatenate([q_ref[:, i * HEAD_DIM:(i + 1) * HEAD_DIM], zeros], axis=1)


def _merge_heads(acc_ref):
    outs = []
    for g in range(GROUP):
        a = acc_ref[g]
        outs.append(a / pltpu.roll(a, HEAD_DIM, axis=1))
    low = _iota((1, LANES), 1) < HEAD_DIM
    return jnp.where(low, outs[0], pltpu.roll(outs[1], HEAD_DIM, axis=1))


def _fox_kernel(qi_tab, kj_tab, q_ref, kv_ref, cq_ref, ck_ref, o_ref, qs_ref, m_ref, acc_ref, *, tile):
    step = pl.program_id(2)
    qi, kj = qi_tab[step], kj_tab[step]

    @pl.when(kj == 0)
    def _():
        _init_flash(m_ref, acc_ref)
        _pad_heads(q_ref, qs_ref, GROUP)

    def update(masked):
        k = kv_ref[:, :LANES]
        vext = kv_ref[:, LANES:]
        for g in range(GROUP):
            bias = (cq_ref[g][:, 0:1] - ck_ref[g]) * LOG2E
            logits = lambda r0, r1, c0, c1: _dot_t(qs_ref[g, r0:r1], k[c0:c1]) + bias[:, c0:c1]
            _tile_update(logits, vext, m_ref, acc_ref, g, tile, masked)

    @pl.when(kj < qi)
    def _():
        update(False)

    @pl.when(kj == qi)
    def _():
        update(True)
        o_ref[...] = _merge_heads(acc_ref).astype(o_ref.dtype)


def _fox_prompt(q_all, kvx, c_t, *, tile):
    b, t, _ = q_all.shape
    qi_tab, kj_tab = _tri_tables(t // tile)
    xb = OFF_XB // KVX_AB
    grid_spec = pltpu.PrefetchScalarGridSpec(
        num_scalar_prefetch=2, grid=(b, KV_B, int(qi_tab.shape[0])),
        in_specs=[
            pl.BlockSpec((None, tile, LANES), lambda bi, h, s, qt, kt: (bi, qt[s], KV_A + h)),
            pl.BlockSpec((None, tile, KVX_AB), lambda bi, h, s, qt, kt: (bi, kt[s], xb + h)),
            pl.BlockSpec((None, GROUP, 1, tile), lambda bi, h, s, qt, kt: (bi, h, 0, qt[s])),
            pl.BlockSpec((None, GROUP, 1, tile), lambda bi, h, s, qt, kt: (bi, h, 0, kt[s])),
        ],
        out_specs=pl.BlockSpec((None, tile, LANES), lambda bi, h, s, qt, kt: (bi, qt[s], h)),
        scratch_shapes=[pltpu.VMEM((GROUP, tile, LANES), BF16), pltpu.VMEM((GROUP, tile, LANES), F32),
                        pltpu.VMEM((GROUP, tile, LANES), F32)])
    return pl.pallas_call(
        functools.partial(_fox_kernel, tile=tile), grid_spec=grid_spec,
        out_shape=jax.ShapeDtypeStruct((b, t, W_BRANCH), BF16), compiler_params=_params(3),
        name="fox_prompt")(qi_tab, kj_tab, q_all, kvx, c_t[:, :, None, :], c_t[:, :, None, :])


def _diff_kernel(qi_tab, kj_tab, slope_ref, q_ref, kv_ref, lam_ref, sg_ref, o_ref, qs_ref, m_ref, acc_ref, *, tile,
                 lam_init):
    hk = pl.program_id(1)
    step = pl.program_id(2)
    qi, kj = qi_tab[step], kj_tab[step]

    @pl.when(kj == 0)
    def _():
        _init_flash(m_ref, acc_ref)
        _pad_heads(q_ref, qs_ref, 2 * GROUP)

    def update(masked):
        vext = kv_ref[:, 2 * LANES:]
        rel = ((kj - qi) * tile + _iota((1, tile), 1)).astype(F32)
        for g in range(GROUP):
            bias = slope_ref[hk * GROUP + g] * rel
            for m in range(2):
                i = m * GROUP + g
                logits = lambda r0, r1, c0, c1: (
                    _dot_t(qs_ref[i, r0:r1], kv_ref[c0:c1, m * LANES:(m + 1) * LANES]) + bias[:, c0:c1])
                _tile_update(logits, vext, m_ref, acc_ref, i, tile, masked)

    @pl.when(kj < qi)
    def _():
        update(False)

    @pl.when(kj == qi)
    def _():
        update(True)
        lam = lam_ref[...]
        lam_val = (jnp.exp(jnp.sum(lam[0:1] * lam[1:2], axis=1, keepdims=True))
                   - jnp.exp(jnp.sum(lam[2:3] * lam[3:4], axis=1, keepdims=True)) + lam_init)
        for g in range(GROUP):
            a0, a1 = acc_ref[g], acc_ref[GROUP + g]
            o = a0[:, :V_DIM_C] / a0[:, V_DIM_C:] - lam_val * (a1[:, :V_DIM_C] / a1[:, V_DIM_C:])
            ms = jnp.mean(o * o, axis=-1, keepdims=True)
            o = (o * lax.rsqrt(ms + RMS_EPS)) * sg_ref[...] * (1.0 - lam_init)
            o_ref[:, g * V_DIM_C:(g + 1) * V_DIM_C] = o.astype(o_ref.dtype)


def _diff_prompt(q_all, kvx, slopes, lam_l, sg_l, *, tile, lam_init):
    b, t, _ = q_all.shape
    qi_tab, kj_tab = _tri_tables(t // tile)
    xc = OFF_XC // KVX_C
    grid_spec = pltpu.PrefetchScalarGridSpec(
        num_scalar_prefetch=2, grid=(b, KV_C, int(qi_tab.shape[0])),
        in_specs=[
            pl.BlockSpec(memory_space=pltpu.SMEM),
            pl.BlockSpec((None, tile, 2 * LANES), lambda bi, h, s, qt, kt: (bi, qt[s], 4 + h)),
            pl.BlockSpec((None, tile, KVX_C), lambda bi, h, s, qt, kt: (bi, kt[s], xc + h)),
            pl.BlockSpec((4, HEAD_DIM), lambda bi, h, s, qt, kt: (0, 0)),
            pl.BlockSpec((1, V_DIM_C), lambda bi, h, s, qt, kt: (0, 0)),
        ],
        out_specs=pl.BlockSpec((None, tile, 2 * LANES), lambda bi, h, s, qt, kt: (bi, qt[s], h)),
        scratch_shapes=[pltpu.VMEM((2 * GROUP, tile, LANES), BF16), pltpu.VMEM((2 * GROUP, tile, LANES), F32),
                        pltpu.VMEM((2 * GROUP, tile, 2 * V_DIM_C), F32)])
    return pl.pallas_call(
        functools.partial(_diff_kernel, tile=tile, lam_init=lam_init), grid_spec=grid_spec,
        out_shape=jax.ShapeDtypeStruct((b, t, W_BRANCH), BF16), compiler_params=_params(3),
        name="diff_prompt")(qi_tab, kj_tab, slopes, q_all, kvx, lam_l, sg_l)


def _topk_keep(gate, blk, own, past, nblk, axis):
    gm = jnp.where(past, gate, -jnp.inf)
    cnt = jnp.zeros(gate.shape, jnp.int32)
    for jp in range(nblk):
        gj = gm[jp:jp + 1, :] if axis == 0 else gm[:, jp:jp + 1]
        ahead = (gj > gm) | ((gj == gm) & (jp < blk))
        cnt = cnt + ahead.astype(jnp.int32)
    return (past & (cnt < MOBA_TOPK)) | (blk == own)


def _moba_kernel(qi_tab, kj_tab, slope_ref, q_ref, kv_ref, km_ref, o_ref, qa_ref, m_ref, acc_ref, *, tile, nblk):
    hk = pl.program_id(1)
    step = pl.program_id(2)
    qi, kj = qi_tab[step], kj_tab[step]
    bpt = tile // MOBA_BLOCK
    nbp = -(-nblk // SUBLANES) * SUBLANES

    @pl.when(kj == 0)
    def _():
        _init_flash(m_ref, acc_ref)
        own = lax.shift_right_logical(qi * tile + _iota((1, tile), 1), _log2(MOBA_BLOCK))
        blk = _iota((nbp, 1), 0)
        zeros = jnp.zeros((tile, HEAD_DIM), BF16)
        for g in range(GROUP):
            qg = q_ref[:, g * HEAD_DIM:(g + 1) * HEAD_DIM]
            gate = _dot_t(km_ref[:nbp, :HEAD_DIM], qg.astype(F32), precision=lax.Precision.HIGHEST)
            keep = _topk_keep(gate, blk, own, blk < own, nblk, 0)
            bias = jnp.where(keep, 0.0, NEG)
            bias = jnp.concatenate([bias, jnp.zeros((LANES - nbp, tile), F32)], axis=0)
            qa_ref[g, :, :LANES] = jnp.concatenate([qg, zeros], axis=1)
            qa_ref[g, :, LANES:] = bias.T.astype(BF16)

    def update(masked):
        key_blk = kj * bpt + lax.shift_right_logical(_iota((tile, 1), 0), _log2(MOBA_BLOCK))
        onehot = jnp.where(_iota((1, LANES), 1) == key_blk, 1.0, 0.0).astype(BF16)
        kaug = jnp.concatenate([kv_ref[:, :LANES], onehot], axis=1)
        vext = kv_ref[:, LANES:]
        rel = ((kj - qi) * tile + _iota((1, tile), 1)).astype(F32)
        for g in range(GROUP):
            bias = slope_ref[hk * GROUP + g] * rel
            logits = lambda r0, r1, c0, c1: _dot_t(qa_ref[g, r0:r1], kaug[c0:c1]) + bias[:, c0:c1]
            _tile_update(logits, vext, m_ref, acc_ref, g, tile, masked)

    @pl.when(kj < qi)
    def _():
        update(False)

    @pl.when(kj == qi)
    def _():
        update(True)
        o_ref[...] = _merge_heads(acc_ref).astype(o_ref.dtype)


def _moba_prompt(q_all, kvx, kmean_pad, slopes, *, tile):
    b, t, _ = q_all.shape
    qi_tab, kj_tab = _tri_tables(t // tile)
    grid_spec = pltpu.PrefetchScalarGridSpec(
        num_scalar_prefetch=2, grid=(b, KV_A, int(qi_tab.shape[0])),
        in_specs=[
            pl.BlockSpec(memory_space=pltpu.SMEM),
            pl.BlockSpec((None, tile, LANES), lambda bi, h, s, qt, kt: (bi, qt[s], h)),
            pl.BlockSpec((None, tile, KVX_AB), lambda bi, h, s, qt, kt: (bi, kt[s], h)),
            pl.BlockSpec((None, LANES, LANES), lambda bi, h, s, qt, kt: (bi, 0, h)),
        ],
        out_specs=pl.BlockSpec((None, tile, LANES), lambda bi, h, s, qt, kt: (bi, qt[s], h)),
        scratch_shapes=[pltpu.VMEM((GROUP, tile, 2 * LANES), BF16), pltpu.VMEM((GROUP, tile, LANES), F32),
                        pltpu.VMEM((GROUP, tile, LANES), F32)])
    return pl.pallas_call(
        functools.partial(_moba_kernel, tile=tile, nblk=t // MOBA_BLOCK), grid_spec=grid_spec,
        out_shape=jax.ShapeDtypeStruct((b, t, W_BRANCH), BF16), compiler_params=_params(3),
        name="moba_prompt")(qi_tab, kj_tab, slopes, q_all, kvx, kmean_pad)


def _rows_of(q_ref, cols):
    return jnp.concatenate([q_ref[:, c:c + HEAD_DIM] for c in cols], axis=0)


def _col_per_group(vals, n_tok):
    return jnp.concatenate([jnp.full((n_tok, 1), v, F32) for v in vals], axis=0)


def _rows_per_group(x, n_tok):
    return jnp.concatenate([jnp.broadcast_to(x[i:i + 1], (n_tok, x.shape[1])) for i in range(x.shape[0])], axis=0)


def _head_match(rows, cols, nkv, n_tok, col_head):
    row_head = lax.shift_right_logical(_iota((rows, 1), 0), _log2(GROUP * n_tok)) & (nkv - 1)
    return jnp.where(col_head == row_head, 0.0, NEG)


def _new_rows(kvn_ref, nkv):
    w = kvn_ref.shape[1] // nkv
    return jnp.concatenate([kvn_ref[:, h * w:(h + 1) * w] for h in range(nkv)], axis=0)


def _new_bias(rows, nkv, n_tok):
    col = _iota((1, nkv * n_tok), 1)
    bias = _head_match(rows, nkv * n_tok, nkv, n_tok, lax.shift_right_logical(col, _log2(n_tok)))
    visible = (col & (n_tok - 1)) <= (_iota((rows, 1), 0) & (n_tok - 1))
    return jnp.where(visible, bias, NEG), (col & (n_tok - 1)).astype(F32)


def _multi_update(s_list, v_list, m_ref, l_ref, acc_ref, shift=0):
    idx = ...
    m_prev = m_ref[idx]
    smax = s_list[0]
    for s in s_list[1:]:
        smax = jnp.maximum(smax, s)
    m_next = jnp.maximum(m_prev, jnp.max(smax, axis=1, keepdims=True))
    alpha = jnp.exp2(m_prev - m_next)
    psum, pv = None, None
    for s, v in zip(s_list, v_list):
        p = jnp.exp2(s - m_next)
        pm = pltpu.roll(p, shift, axis=1) if shift else p
        d = jnp.dot(pm.astype(BF16), v, preferred_element_type=F32)
        psum = p if psum is None else psum + p
        pv = d if pv is None else pv + d
    l_ref[idx] = alpha * l_ref[idx] + jnp.sum(psum, axis=1, keepdims=True)
    acc_ref[idx] = alpha * acc_ref[idx] + pv
    m_ref[idx] = m_next


def _init_multi(m_ref, l_ref, acc_ref):
    m_ref[...] = jnp.full(m_ref.shape, NEG, F32)
    l_ref[...] = jnp.zeros(l_ref.shape, F32)
    acc_ref[...] = jnp.zeros(acc_ref.shape, F32)


def _fox_sample_kernel(pt_ref, q_ref, kvn_ref, lfn_ref, *rest, npg, n_tok):
    pages, lfs = rest[:npg], rest[npg:2 * npg]
    o_ref, m_ref, l_ref, acc_ref, carry_ref = rest[2 * npg:]
    j = pl.program_id(1)
    rows, pr = H_B * n_tok, PAGE_SIZE * KV_B

    @pl.when(j == 0)
    def _():
        _init_multi(m_ref, l_ref, acc_ref)
        carry_ref[...] = jnp.zeros(carry_ref.shape, F32)

    q = _rows_of(q_ref, [h * HEAD_DIM for h in range(H_B)]).astype(BF16)
    col = _iota((1, pr), 1)
    match = _head_match(rows, pr, KV_B, n_tok, col & (KV_B - 1))
    upper = (_iota((PAGE_SIZE, 1), 0) <= lax.shift_right_logical(col, _log2(KV_B))).astype(F32)
    carry = carry_ref[...]
    s_list, v_list = [], []
    c_local = jnp.dot(jnp.concatenate([lf[...] for lf in lfs], axis=0), upper, preferred_element_type=F32,
                      precision=lax.Precision.HIGHEST)
    for r in range(npg):
        c_full = c_local[r * H_B:(r + 1) * H_B] + carry
        carry = c_full[:, pr - 1:pr]
        bias = match - _rows_per_group(c_full, n_tok) * LOG2E
        s_list.append(_dot_t(q, pages[r][:, :HEAD_DIM].astype(BF16)) + bias)
        v_list.append(pages[r][:, HEAD_DIM:].astype(BF16))
    carry_ref[...] = carry
    _multi_update(s_list, v_list, m_ref, l_ref, acc_ref)

    @pl.when(j == pl.num_programs(1) - 1)
    def _():
        kvn = _new_rows(kvn_ref, KV_B)
        bias, _ = _new_bias(rows, KV_B, n_tok)
        tok = _iota((1, KV_B * n_tok), 1) & (n_tok - 1)
        upper_n = (_iota((n_tok, 1), 0) <= tok).astype(F32)
        c_new = _dot_tl(lfn_ref[...], upper_n) + carry
        s = _dot_t(q, kvn[:, :HEAD_DIM].astype(BF16)) + (bias - _rows_per_group(c_new, n_tok) * LOG2E)
        _multi_update([s], [kvn[:, HEAD_DIM:].astype(BF16)], m_ref, l_ref, acc_ref)
        o = acc_ref[...] / l_ref[...]
        for h in range(H_B):
            o_ref[:, h * HEAD_DIM:(h + 1) * HEAD_DIM] = o[h * n_tok:(h + 1) * n_tok]


def _diff_sample_kernel(pt_ref, slope_ref, lam_ref, sg_ref, q_ref, kvn_ref, *rest, npg, n_tok, lam_init):
    pages = rest[:npg]
    o_ref, m_ref, l_ref, acc_ref = rest[npg:]
    j = pl.program_id(1)
    half = H_C * n_tok
    rows, pr = 2 * half, 2 * PAGE_SIZE * KV_C
    n_past = pl.num_programs(1) * npg * PAGE_SIZE

    @pl.when(j == 0)
    def _():
        _init_multi(m_ref, l_ref, acc_ref)

    qs = [_rows_of(q_ref, [((hk * 2 + m) * GROUP + g) * HEAD_DIM for hk in range(KV_C) for g in range(GROUP)])
          .astype(BF16) for m in range(2)]
    slope = _col_per_group([slope_ref[h] for h in range(H_C)] * 2, n_tok)

    def logits(kv, bias):
        return jnp.concatenate([_dot_t(qs[m], kv[:, m * HEAD_DIM:(m + 1) * HEAD_DIM].astype(BF16))
                                for m in range(2)], axis=0) + bias

    col = _iota((1, pr), 1)
    key_head = jnp.where((col & KV_C) == 0, col & (KV_C - 1), -1)
    match = _head_match(rows, pr, KV_C, n_tok, key_head)
    tok = lax.shift_right_logical(col, _log2(2 * KV_C))
    s_list, v_list = [], []
    for r in range(npg):
        rel = ((j * npg + r) * PAGE_SIZE - n_past + tok).astype(F32)
        s_list.append(logits(pages[r], match + slope * rel))
        v_list.append(pages[r][...].astype(BF16))
    _multi_update(s_list, v_list, m_ref, l_ref, acc_ref, shift=KV_C)

    @pl.when(j == pl.num_programs(1) - 1)
    def _():
        kvn = _new_rows(kvn_ref, KV_C)
        bias, rel = _new_bias(rows, KV_C, n_tok)
        _multi_update([logits(kvn, bias + slope * rel)], [kvn[:, 2 * HEAD_DIM:].astype(BF16)], m_ref, l_ref, acc_ref)
        lam = lam_ref[...]
        lam_val = (jnp.exp(jnp.sum(lam[0:1] * lam[1:2], axis=1, keepdims=True))
                   - jnp.exp(jnp.sum(lam[2:3] * lam[3:4], axis=1, keepdims=True)) + lam_init)
        on = acc_ref[...] / l_ref[...]
        o = on[:half] - lam_val * on[half:]
        ms = jnp.mean(o * o, axis=-1, keepdims=True)
        o = (o * lax.rsqrt(ms + RMS_EPS)) * sg_ref[...] * (1.0 - lam_init)
        for h in range(H_C):
            o_ref[:, h * V_DIM_C:(h + 1) * V_DIM_C] = o[h * n_tok:(h + 1) * n_tok]


def _moba_sample_kernel(pt_ref, slope_ref, q_ref, kvn_ref, *rest, npg, n_tok, nblk):
    pages = rest[:npg]
    o_ref, mb_ref, lb_ref, gb_ref, ob_ref = rest[npg:]
    j = pl.program_id(1)
    rows, pr = H_A * n_tok, PAGE_SIZE * KV_A
    ppb = MOBA_BLOCK // PAGE_SIZE
    bps = npg // ppb
    lane = _iota((1, LANES), 1)
    n_past = pl.num_programs(1) * npg * PAGE_SIZE

    @pl.when(j == 0)
    def _():
        mb_ref[...] = jnp.full(mb_ref.shape, NEG, F32)
        lb_ref[...] = jnp.zeros(lb_ref.shape, F32)
        gb_ref[...] = jnp.zeros(gb_ref.shape, F32)

    qf = _rows_of(q_ref, [h * HEAD_DIM for h in range(H_A)])
    q = qf.astype(BF16)
    slope = _col_per_group([slope_ref[h] for h in range(H_A)], n_tok)
    col = _iota((1, pr), 1)
    match = _head_match(rows, pr, KV_A, n_tok, col & (KV_A - 1))
    tok = lax.shift_right_logical(col, _log2(KV_A))
    row_kv = lax.shift_right_logical(_iota((rows, 1), 0), _log2(GROUP * n_tok))

    for bl in range(bps):
        blk = j * bps + bl
        s_l, v_l, kmean = [], [], None
        for r in range(ppb):
            page = pages[bl * ppb + r]
            rel = ((blk * ppb + r) * PAGE_SIZE - n_past + tok).astype(F32)
            s_l.append(_dot_t(q, page[:, :HEAD_DIM].astype(BF16)) + (match + slope * rel))
            v_l.append(page[:, HEAD_DIM:].astype(BF16))
            km = jnp.sum(page[:, :HEAD_DIM].reshape(pr // SUBLANES, SUBLANES, HEAD_DIM), axis=0)
            kmean = km if kmean is None else kmean + km
        mb = jnp.max(functools.reduce(jnp.maximum, s_l), axis=1, keepdims=True)
        ps = [jnp.exp2(s - mb) for s in s_l]
        lb = jnp.sum(functools.reduce(jnp.add, ps), axis=1, keepdims=True)
        ob_ref[blk] = functools.reduce(
            jnp.add, [jnp.dot(p.astype(BF16), v, preferred_element_type=F32) for p, v in zip(ps, v_l)])
        kmean = (kmean[:KV_A] + kmean[KV_A:]) * (1.0 / MOBA_BLOCK)
        kmean = jnp.concatenate([kmean, jnp.zeros((SUBLANES - KV_A, HEAD_DIM), F32)], axis=0)
        gate_all = _dot_t(qf, kmean, precision=lax.Precision.HIGHEST)
        gate = jnp.sum(jnp.where(_iota((1, SUBLANES), 1) == row_kv, gate_all, 0.0), axis=1, keepdims=True)
        here = lane == blk
        mb_ref[...] = jnp.where(here, mb, mb_ref[...])
        lb_ref[...] = jnp.where(here, lb, lb_ref[...])
        gb_ref[...] = jnp.where(here, gate, gb_ref[...])

    @pl.when(j == pl.num_programs(1) - 1)
    def _():
        keep = _topk_keep(gb_ref[...], lane, nblk, lane < nblk, nblk, 1) & (lane < nblk)
        m_sel = jnp.where(keep, mb_ref[...], NEG)
        kvn = _new_rows(kvn_ref, KV_A)
        bias, rel = _new_bias(rows, KV_A, n_tok)
        s = _dot_t(q, kvn[:, :HEAD_DIM].astype(BF16)) + (bias + slope * rel)
        m_tot = jnp.maximum(jnp.max(s, axis=1, keepdims=True), jnp.max(m_sel, axis=1, keepdims=True))
        p = jnp.exp2(s - m_tot)
        w = jnp.where(keep, jnp.exp2(m_sel - m_tot), 0.0)
        den = jnp.sum(p, axis=1, keepdims=True) + jnp.sum(w * lb_ref[...], axis=1, keepdims=True)
        num = jnp.dot(p.astype(BF16), kvn[:, HEAD_DIM:].astype(BF16), preferred_element_type=F32)
        for b in range(nblk):
            num = num + w[:, b:b + 1] * ob_ref[b]
        o = num / den
        for h in range(H_A):
            o_ref[:, h * HEAD_DIM:(h + 1) * HEAD_DIM] = o[h * n_tok:(h + 1) * n_tok]


def _page_specs(layer, n_pages, npg, page_shape):
    zeros = (0,) * len(page_shape)

    def spec(r):
        return pl.BlockSpec((None, None) + tuple(page_shape),
                            lambda b, j, pt: (layer, pt[b * n_pages + j * npg + r]) + zeros)
    return [spec(r) for r in range(npg)]


def _sample_attention(kind, layer, pt_flat, q_s, kvn, cache, *, n_pages, extra=(), extra_specs=(), logf_cache=None,
                      lfn=None, lam_init=0.0):
    db, n_tok, _ = q_s.shape
    npg = PAGES_PER_STEP if n_pages % PAGES_PER_STEP == 0 else MOBA_BLOCK // PAGE_SIZE
    per_b = lambda w: pl.BlockSpec((None, n_tok, w), lambda b, j, pt: (b, 0, 0))
    qcol = {"moba": 0, "fox": 1, "diff": 2}[kind]
    q_spec = pl.BlockSpec((None, n_tok, W_BRANCH), lambda b, j, pt: (b, 0, qcol))
    in_specs = list(extra_specs) + [q_spec, per_b(W_BRANCH)]
    args = list(extra) + [q_s, kvn]
    if kind == "fox":
        in_specs.append(per_b(H_B))
        args.append(lfn)
    in_specs += _page_specs(layer, n_pages, npg, cache.shape[2:])
    args += [cache] * npg
    col = lambda r, w: pltpu.VMEM((r, w), F32)
    if kind == "fox":
        rows = H_B * n_tok
        in_specs += _page_specs(layer, n_pages, npg, (H_B, PAGE_SIZE))
        args += [logf_cache] * npg
        kern = functools.partial(_fox_sample_kernel, npg=npg, n_tok=n_tok)
        scratch = [col(rows, 1), col(rows, 1), col(rows, HEAD_DIM), col(H_B, 1)]
    elif kind == "diff":
        rows = 2 * H_C * n_tok
        kern = functools.partial(_diff_sample_kernel, npg=npg, n_tok=n_tok, lam_init=lam_init)
        scratch = [col(rows, 1), col(rows, 1), col(rows, V_DIM_C)]
    else:
        rows = H_A * n_tok
        nblk = n_pages * PAGE_SIZE // MOBA_BLOCK
        kern = functools.partial(_moba_sample_kernel, npg=npg, n_tok=n_tok, nblk=nblk)
        scratch = [col(rows, LANES), col(rows, LANES), col(rows, LANES), pltpu.VMEM((nblk, rows, HEAD_DIM), F32)]
    grid_spec = pltpu.PrefetchScalarGridSpec(
        num_scalar_prefetch=1, grid=(db, n_pages // npg), in_specs=in_specs, out_specs=per_b(W_BRANCH),
        scratch_shapes=scratch)
    return pl.pallas_call(kern, grid_spec=grid_spec, out_shape=jax.ShapeDtypeStruct((db, n_tok, W_BRANCH), F32),
                          compiler_params=_params(2), name=kind + "_sample")(pt_flat, *args)


def _alibi_slopes(n):
    return 2.0 ** (-8.0 * jnp.arange(1, n + 1, dtype=F32) / n)


def kernel(x_prompt, x_sample, cache_a_kv, cache_b_kv, cache_b_logf, cache_c_kv, page_table, w_in, w_o_a, w_o_b,
           w_o_c, w_out, norm_g, forget_b, diff_lambda, diff_subln_g, final_norm_g):
    depth = w_in.shape[0]
    bp, t, _ = x_prompt.shape
    db, n_tok, _ = x_sample.shape
    n_pages = page_table.shape[1]
    n_pool = cache_a_kv.shape[1]
    assert t % MOBA_BLOCK == 0 and (n_pages * PAGE_SIZE) % MOBA_BLOCK == 0
    assert t // MOBA_BLOCK <= LANES and n_pages * PAGE_SIZE // MOBA_BLOCK <= LANES
    assert n_tok == SUBLANES

    w_perm = _permute_columns(w_in)
    woa, wob, woc, wout = (w.astype(BF16) for w in (w_o_a, w_o_b, w_o_c, w_out))
    norm_g3 = norm_g.reshape(depth, 1, D_MODEL)
    fb3 = jnp.pad(forget_b, ((0, 0), (0, FB_PAD - H_B))).reshape(depth, 1, FB_PAD)
    fg2 = final_norm_g.reshape(1, D_MODEL)
    slopes_a, slopes_c = _alibi_slopes(H_A) * LOG2E, _alibi_slopes(H_C) * LOG2E
    pt_flat = page_table.reshape(-1).astype(jnp.int32)
    ca = cache_a_kv.reshape(depth, n_pool, PAGE_SIZE * KV_A, 2 * HEAD_DIM)
    cb = cache_b_kv.reshape(depth, n_pool, PAGE_SIZE * KV_B, 2 * HEAD_DIM)
    logf_t = jnp.swapaxes(cache_b_logf, 2, 3)
    cc = cache_c_kv.reshape(depth, n_pool, PAGE_SIZE, KV_C, 2, LANES).transpose(0, 1, 2, 4, 3, 5)
    cc = cc.reshape(depth, n_pool, 2 * PAGE_SIZE * KV_C, LANES)

    tm_p = 256 if (bp * t) % 256 == 0 else bp * t
    tile = next((c for c in (1024, 512) if t % c == 0), t)
    ns = db * n_tok
    smem = pl.BlockSpec(memory_space=pltpu.SMEM)

    xp = x_prompt.reshape(bp * t, D_MODEL)
    xs = x_sample.reshape(ns, D_MODEL)
    outs_p = [[], [], [], []]
    outs_s = [[], [], [], []]
    for l in range(depth):
        lam_init = 0.8 - 0.6 * math.exp(-0.3 * l)
        sg = diff_subln_g[l].reshape(1, V_DIM_C)
        final = l == depth - 1

        kva, kvb, kvc, logf, kvx, q_all, gates, gm = _inproj(xp, l, norm_g3, fb3, w_perm, tm=tm_p, q_dtype=BF16)
        for acc, v in zip(outs_p, (kva, kvb, logf, kvc)):
            acc.append(v)
        q3 = q_all.reshape(bp, t, 3 * W_BRANCH)
        kv3 = kvx.reshape(bp, t, W_KVX)
        km = _kmean(kva).reshape(bp, t // MOBA_BLOCK, W_BRANCH)
        km = jnp.pad(km, ((0, 0), (0, LANES - t // MOBA_BLOCK), (0, 0)))
        c_t = _cumsum_t(jnp.swapaxes(logf.reshape(bp, t, H_B), 1, 2))
        oa = _moba_prompt(q3, kv3, km, slopes_a, tile=tile)
        ob = _fox_prompt(q3, kv3, c_t, tile=tile)
        oc = _diff_prompt(q3, kv3, slopes_c, diff_lambda[l], sg, tile=tile, lam_init=lam_init)
        xp = _outproj(xp, oa.reshape(bp * t, W_BRANCH), ob.reshape(bp * t, W_BRANCH), oc.reshape(bp * t, W_BRANCH),
                      gates, gm, l, woa, wob, woc, wout, fg2, tm=tm_p, final=final)

        kva, kvb, kvc, logf, _, q_all, gates, gm = _inproj(xs, l, norm_g3, fb3, w_perm, tm=ns, q_dtype=F32)
        for acc, v in zip(outs_s, (kva, kvb, logf, kvc)):
            acc.append(v)
        q3 = q_all.reshape(db, n_tok, 3 * W_BRANCH)
        r3 = lambda a: a.reshape(db, n_tok, a.shape[-1])
        oa = _sample_attention("moba", l, pt_flat, q3, r3(kva), ca, n_pages=n_pages, extra=(slopes_a,),
                               extra_specs=(smem,))
        ob = _sample_attention("fox", l, pt_flat, q3, r3(kvb), cb, n_pages=n_pages, logf_cache=logf_t,
                               lfn=r3(logf))
        oc = _sample_attention("diff", l, pt_flat, q3, r3(kvc), cc, n_pages=n_pages,
                               extra=(slopes_c, diff_lambda[l], sg),
                               extra_specs=(smem, pl.BlockSpec((4, HEAD_DIM), lambda b, j, pt: (0, 0)),
                                            pl.BlockSpec((1, V_DIM_C), lambda b, j, pt: (0, 0))),
                               lam_init=lam_init)
        xs = _outproj(xs, oa.reshape(ns, W_BRANCH), ob.reshape(ns, W_BRANCH), oc.reshape(ns, W_BRANCH), gates, gm, l,
                      woa, wob, woc, wout, fg2, tm=ns, final=final)

    def stack(vals, b_, t_, tail):
        return jnp.stack(vals).reshape((depth, b_, t_) + tail)

    pa, pb, pf, pc = outs_p
    sa, sb, sf, sc = outs_s
    kv_tail, c_tail = (KV_A, 2 * HEAD_DIM), (KV_C, 2 * HEAD_DIM + V_DIM_C)
    return (xp.reshape(bp, t, D_MODEL), xs.reshape(db, n_tok, D_MODEL),
            stack(pa, bp, t, kv_tail), stack(pb, bp, t, kv_tail), stack(pf, bp, t, (H_B,)), stack(pc, bp, t, c_tail),
            stack(sa, db, n_tok, kv_tail), stack(sb, db, n_tok, kv_tail), stack(sf, db, n_tok, (H_B,)),
            stack(sc, db, n_tok, c_tail))
```

```python
import functools
import math

import jax
import jax.numpy as jnp
import numpy as np
from jax import lax
from jax.experimental import pallas as pl
from jax.experimental.pallas import tpu as pltpu

F32 = jnp.float32
BF16 = jnp.bfloat16

D_MODEL = 1024
HEAD_DIM = 64
PAGE_SIZE = 128
H_A, KV_A = 8, 4
H_B, KV_B = 8, 4
H_C, KV_C = 4, 2
GROUP = 2
V_DIM_C = 2 * HEAD_DIM
MOBA_BLOCK = 256
MOBA_TOPK = 3
N_BRANCH = 3
RMS_EPS = 1e-6
W_BRANCH = 512
IN_NAMES = ("qa", "ka", "va", "ga", "qb", "kb", "vb", "fb", "gb", "qc", "kc", "vc", "gc", "gm")
IN_SIZES = (512, 256, 256, 512, 512, 256, 256, 8, 512, 512, 256, 256, 512, N_BRANCH * D_MODEL)
D_IN = sum(IN_SIZES)

NEG = -1e30
LOG2E = math.log2(math.e)
LANES = 128
SUBLANES = 8
FB_PAD = LANES
OFF_KVA, OFF_KVB, OFF_KVC, OFF_FB = 0, 512, 1024, 1536
OFF_Q = OFF_FB + FB_PAD
OFF_G = OFF_Q + 3 * W_BRANCH
OFF_GM = OFF_G + 3 * W_BRANCH
D_PERM = OFF_GM + N_BRANCH * D_MODEL
KVX_AB = 4 * HEAD_DIM
KVX_C = 8 * HEAD_DIM
OFF_XA, OFF_XB, OFF_XC = 0, KV_A * KVX_AB, KV_A * KVX_AB + KV_B * KVX_AB
W_KVX = OFF_XC + KV_C * KVX_C
VMEM_LIMIT = 56 * 1024 * 1024
Q_TILE_MULT = 2
PAGES_PER_STEP = 32


def _perm_and_scale():
    off = dict(zip(IN_NAMES, np.concatenate([[0], np.cumsum(IN_SIZES)[:-1]])))
    d = HEAD_DIM
    cols = []
    for kn, vn, nkv in (("ka", "va", KV_A), ("kb", "vb", KV_B)):
        for h in range(nkv):
            cols += list(range(off[kn] + h * d, off[kn] + (h + 1) * d))
            cols += list(range(off[vn] + h * d, off[vn] + (h + 1) * d))
    for h in range(KV_C):
        for m in range(2):
            cols += list(range(off["kc"] + (m * KV_C + h) * d, off["kc"] + (m * KV_C + h + 1) * d))
        cols += list(range(off["vc"] + h * V_DIM_C, off["vc"] + (h + 1) * V_DIM_C))
    cols += list(range(off["fb"], off["fb"] + H_B)) + [D_IN] * (FB_PAD - H_B)
    cols += list(range(off["qa"], off["qa"] + 512))
    cols += list(range(off["qb"], off["qb"] + 512))
    for h in range(KV_C):
        for m in range(2):
            for g in range(GROUP):
                s = off["qc"] + ((m * KV_C + h) * GROUP + g) * d
                cols += list(range(s, s + d))
    for n in ("ga", "gb", "gc"):
        cols += list(range(off[n], off[n] + 512))
    cols += list(range(off["gm"], off["gm"] + N_BRANCH * D_MODEL))
    perm = np.asarray(cols, np.int32)
    assert perm.shape[0] == D_PERM
    scale = np.ones((D_PERM,), np.float32)
    scale[OFF_Q:OFF_G] = HEAD_DIM ** -0.5 * LOG2E
    return perm, scale


_PERM, _SCALE = _perm_and_scale()


def _permute_columns(w_in):
    pieces, start = [], 0
    for i in range(1, D_PERM + 1):
        joined = i < D_PERM and _SCALE[i] == _SCALE[start] and (
            _PERM[i] == _PERM[i - 1] + 1 or _PERM[i] == _PERM[i - 1] == D_IN)
        if not joined:
            a, n = int(_PERM[start]), i - start
            if a == D_IN:
                pieces.append(jnp.zeros(w_in.shape[:2] + (n,), BF16))
            else:
                pieces.append((w_in[:, :, a:a + n] * float(_SCALE[start])).astype(BF16))
            start = i
    return jnp.concatenate(pieces, axis=2)


def _params(n_axes):
    return pltpu.CompilerParams(dimension_semantics=("arbitrary",) * n_axes, vmem_limit_bytes=VMEM_LIMIT)


def _sigmoid(x):
    return 1.0 / (1.0 + jnp.exp(-x))


def _dot_t(a, b, precision=None):
    return lax.dot_general(a, b, (((1,), (1,)), ((), ())), preferred_element_type=F32, precision=precision)


def _dot_tl(a, b):
    return lax.dot_general(a, b, (((0,), (0,)), ((), ())), preferred_element_type=F32,
                           precision=lax.Precision.HIGHEST)


def _iota(shape, axis):
    return lax.broadcasted_iota(jnp.int32, shape, axis)


def _log2(n):
    assert n & (n - 1) == 0
    return n.bit_length() - 1


def _inproj_kernel(x_ref, g_ref, fb_ref, w_ref, kva_ref, kvb_ref, kvc_ref, logf_ref, kvx_ref, q_ref, gate_ref,
                   gm_ref):
    x = x_ref[...]
    ms = jnp.mean(x * x, axis=-1, keepdims=True)
    h = ((x * lax.rsqrt(ms + RMS_EPS)) * g_ref[...]).astype(BF16)

    def mm(a, n):
        return jnp.dot(h, w_ref[:, a:a + n], preferred_element_type=F32)

    low = (_iota((1, W_BRANCH), 1) & (LANES - 1)) < HEAD_DIM
    for i, (ref, off_x) in enumerate(((kva_ref, OFF_XA), (kvb_ref, OFF_XB))):
        z = mm(i * W_BRANCH, W_BRANCH)
        ref[...] = z
        kx = jnp.where(low, z, 0.0).astype(BF16)
        vx = jnp.where(low, pltpu.roll(z, W_BRANCH - HEAD_DIM, axis=1), 1.0).astype(BF16)
        for hk in range(KV_A):
            kvx_ref[:, off_x + hk * KVX_AB: off_x + hk * KVX_AB + LANES] = kx[:, hk * LANES:(hk + 1) * LANES]
            kvx_ref[:, off_x + hk * KVX_AB + LANES: off_x + (hk + 1) * KVX_AB] = vx[:, hk * LANES:(hk + 1) * LANES]
    z = mm(OFF_KVC, W_BRANCH)
    kvc_ref[...] = z
    k1 = jnp.where(low, z, 0.0).astype(BF16)
    k2 = jnp.where(low, pltpu.roll(z, W_BRANCH - HEAD_DIM, axis=1), 0.0).astype(BF16)
    for hk in range(KV_C):
        c, o = hk * 2 * LANES, OFF_XC + hk * KVX_C
        kvx_ref[:, o: o + LANES] = k1[:, c:c + LANES]
        kvx_ref[:, o + LANES: o + 2 * LANES] = k2[:, c:c + LANES]
        kvx_ref[:, o + 2 * LANES: o + 3 * LANES] = z[:, c + LANES: c + 2 * LANES].astype(BF16)
        kvx_ref[:, o + 3 * LANES: o + 4 * LANES] = jnp.ones((z.shape[0], LANES), BF16)
    zf = mm(OFF_FB, FB_PAD) + fb_ref[...]
    lf = jnp.minimum(zf, 0.0) - jnp.log(1.0 + jnp.exp(-jnp.abs(zf)))
    logf_ref[...] = lf[:, :H_B]
    for i in range(3):
        q_ref[:, i * W_BRANCH:(i + 1) * W_BRANCH] = mm(OFF_Q + i * W_BRANCH, W_BRANCH).astype(q_ref.dtype)
        gate_ref[:, i * W_BRANCH:(i + 1) * W_BRANCH] = mm(OFF_G + i * W_BRANCH, W_BRANCH).astype(BF16)
    for i in range(N_BRANCH * D_MODEL // W_BRANCH):
        gm_ref[:, i * W_BRANCH:(i + 1) * W_BRANCH] = mm(OFF_GM + i * W_BRANCH, W_BRANCH).astype(BF16)


def _inproj(x2d, layer, norm_g3, fb3, w_perm, *, tm, q_dtype):
    n = x2d.shape[0]
    row = lambda w: pl.BlockSpec((tm, w), lambda i: (i, 0))
    lay = lambda shp: pl.BlockSpec((None,) + shp, lambda i: (layer, 0, 0))
    out_shape = (
        jax.ShapeDtypeStruct((n, W_BRANCH), F32), jax.ShapeDtypeStruct((n, W_BRANCH), F32),
        jax.ShapeDtypeStruct((n, W_BRANCH), F32), jax.ShapeDtypeStruct((n, H_B), F32),
        jax.ShapeDtypeStruct((n, W_KVX), BF16), jax.ShapeDtypeStruct((n, 3 * W_BRANCH), q_dtype),
        jax.ShapeDtypeStruct((n, 3 * W_BRANCH), BF16), jax.ShapeDtypeStruct((n, N_BRANCH * D_MODEL), BF16))
    return pl.pallas_call(
        _inproj_kernel, grid=(n // tm,),
        in_specs=[row(D_MODEL), lay((1, D_MODEL)), lay((1, FB_PAD)), lay((D_MODEL, D_PERM))],
        out_specs=[row(W_BRANCH), row(W_BRANCH), row(W_BRANCH), row(H_B), row(W_KVX), row(3 * W_BRANCH),
                   row(3 * W_BRANCH), row(N_BRANCH * D_MODEL)],
        out_shape=out_shape, compiler_params=_params(1), name="inproj")(x2d, norm_g3, fb3, w_perm)


def _outproj_kernel(x_ref, oa_ref, ob_ref, oc_ref, gate_ref, gm_ref, woa_ref, wob_ref, woc_ref, wout_ref, fg_ref,
                    y_ref, *, final):
    def branch(i, o_ref, w_ref):
        g = gate_ref[:, i * W_BRANCH:(i + 1) * W_BRANCH].astype(F32)
        u = (o_ref[...].astype(F32) * (g * _sigmoid(g))).astype(BF16)
        p = jnp.dot(u, w_ref[...], preferred_element_type=F32)
        return _sigmoid(gm_ref[:, i * D_MODEL:(i + 1) * D_MODEL].astype(F32)) * p

    merged = branch(0, oa_ref, woa_ref) + branch(1, ob_ref, wob_ref) + branch(2, oc_ref, woc_ref)
    y = x_ref[...] + jnp.dot(merged.astype(BF16), wout_ref[...], preferred_element_type=F32)
    if final:
        ms = jnp.mean(y * y, axis=-1, keepdims=True)
        y = (y * lax.rsqrt(ms + RMS_EPS)) * fg_ref[...]
    y_ref[...] = y


def _outproj(x2d, oa, ob, oc, gates, gm, layer, woa, wob, woc, wout, fg2, *, tm, final):
    n = x2d.shape[0]
    row = lambda w: pl.BlockSpec((tm, w), lambda i: (i, 0))
    lay = lambda shp: pl.BlockSpec((None,) + shp, lambda i: (layer, 0, 0))
    return pl.pallas_call(
        functools.partial(_outproj_kernel, final=final), grid=(n // tm,),
        in_specs=[row(D_MODEL), row(W_BRANCH), row(W_BRANCH), row(W_BRANCH), row(3 * W_BRANCH),
                  row(N_BRANCH * D_MODEL), lay((W_BRANCH, D_MODEL)), lay((W_BRANCH, D_MODEL)),
                  lay((W_BRANCH, D_MODEL)), lay((D_MODEL, D_MODEL)), pl.BlockSpec((1, D_MODEL), lambda i: (0, 0))],
        out_specs=row(D_MODEL), out_shape=jax.ShapeDtypeStruct((n, D_MODEL), F32),
        compiler_params=_params(1), name="outproj")(x2d, oa, ob, oc, gates, gm, woa, wob, woc, wout, fg2)


def _kmean_kernel(kv_ref, o_ref, *, nb):
    for i in range(nb):
        o_ref[i:i + 1, :] = jnp.mean(kv_ref[i * MOBA_BLOCK:(i + 1) * MOBA_BLOCK, :], axis=0, keepdims=True)


def _kmean(kva2d):
    nblk = kva2d.shape[0] // MOBA_BLOCK
    nb = SUBLANES if nblk % SUBLANES == 0 else nblk
    return pl.pallas_call(
        functools.partial(_kmean_kernel, nb=nb), grid=(nblk // nb,),
        in_specs=[pl.BlockSpec((nb * MOBA_BLOCK, W_BRANCH), lambda i: (i, 0))],
        out_specs=pl.BlockSpec((nb, W_BRANCH), lambda i: (i, 0)),
        out_shape=jax.ShapeDtypeStruct((nblk, W_BRANCH), F32), compiler_params=_params(1), name="kmean")(kva2d)


def _cumsum_kernel(x_ref, o_ref, *, chunk):
    t = x_ref.shape[1]
    u = (_iota((chunk, chunk), 0) <= _iota((chunk, chunk), 1)).astype(F32)

    def body(i, carry):
        st = pl.multiple_of(i * chunk, chunk)
        seg = x_ref[:, pl.ds(st, chunk)]
        cs = jnp.dot(seg, u, preferred_element_type=F32, precision=lax.Precision.HIGHEST) + carry
        o_ref[:, pl.ds(st, chunk)] = cs
        return cs[:, chunk - 1:chunk]

    lax.fori_loop(0, t // chunk, body, jnp.zeros((x_ref.shape[0], 1), F32))


def _cumsum_t(xt):
    b, h, t = xt.shape
    chunk = 256 if t % 256 == 0 else t
    spec = pl.BlockSpec((None, h, t), lambda i: (i, 0, 0))
    return pl.pallas_call(functools.partial(_cumsum_kernel, chunk=chunk), grid=(b,), in_specs=[spec], out_specs=spec,
                          out_shape=jax.ShapeDtypeStruct(xt.shape, F32), compiler_params=_params(1),
                          name="cumsum")(xt)


def _tri_tables(nq, kpq):
    qs, ks = [], []
    for qi in range(nq):
        for kj in range((qi + 1) * kpq):
            qs.append(qi)
            ks.append(kj)
    return jnp.asarray(np.asarray(qs, np.int32)), jnp.asarray(np.asarray(ks, np.int32))


def _flash_update(s, vext, m_ref, acc_ref, idx, r0=0):
    rows = slice(r0, r0 + s.shape[0])
    m_prev = m_ref[idx, rows]
    m_next = jnp.maximum(m_prev, jnp.max(s, axis=1, keepdims=True))
    alpha = jnp.exp2(m_prev - m_next)
    p = jnp.exp2(s - jnp.tile(m_next, (1, s.shape[1] // LANES)))
    pv = jnp.dot(p.astype(BF16), vext, preferred_element_type=F32)
    acc_ref[idx, rows] = jnp.tile(alpha, (1, pv.shape[1] // LANES)) * acc_ref[idx, rows] + pv
    m_ref[idx, rows] = m_next


def _init_flash(m_ref, acc_ref):
    m_ref[...] = jnp.full(m_ref.shape, NEG, F32)
    acc_ref[...] = jnp.zeros(acc_ref.shape, F32)


def _tile_update(logits, vext, m_ref, acc_ref, idx, tq, tk, diag):
    if diag is None:
        _flash_update(logits(0, tq, 0, tk), vext, m_ref, acc_ref, idx)
        return
    h = tk // 2
    for r0, c0, c1 in ((diag, 0, h), (diag + h, h, tk)):
        s = logits(r0, tq, c0, c1)
        s = jnp.where(_iota(s.shape, 1) <= _iota(s.shape, 0), s, NEG)
        _flash_update(s, vext[c0:c1], m_ref, acc_ref, idx, r0=r0)


def _when_tile(qi, kj, kpq, update, finalize):
    @pl.when(kj < qi * kpq)
    def _():
        update(None)

    for d in range(kpq):
        @pl.when(kj == qi * kpq + d)
        def _():
            update(d)
            if d == kpq - 1:
                finalize()


def _pad_heads(q_ref, qs_ref, n):
    zeros = jnp.zeros((q_ref.shape[0], HEAD_DIM), BF16)
    for i in range(n):
        qs_ref[i] = jnp.concatenate([q_ref[:, i * HEAD_DIM:(i + 1) * HEAD_DIM], zeros], axis=1)


def _merge_heads(acc_ref):
    outs = []
    for g in range(GROUP):
        a = acc_ref[g]
        outs.append(a / pltpu.roll(a, HEAD_DIM, axis=1))
    low = _iota((1, LANES), 1) < HEAD_DIM
    return jnp.where(low, outs[0], pltpu.roll(outs[1], HEAD_DIM, axis=1))


def _fox_kernel(qi_tab, kj_tab, q_ref, kv_ref, cq_ref, ck_ref, o_ref, qs_ref, m_ref, acc_ref, *, tq, tk):
    step = pl.program_id(2)
    qi, kj = qi_tab[step], kj_tab[step]

    @pl.when(kj == 0)
    def _():
        _init_flash(m_ref, acc_ref)
        _pad_heads(q_ref, qs_ref, GROUP)

    def update(d):
        k = kv_ref[:, :LANES]
        vext = kv_ref[:, LANES:]
        for g in range(GROUP):
            bias = (cq_ref[g][:, 0:1] - ck_ref[g]) * LOG2E
            logits = lambda r0, r1, c0, c1: _dot_t(qs_ref[g, r0:r1], k[c0:c1]) + bias[:, c0:c1]
            _tile_update(logits, vext, m_ref, acc_ref, g, tq, tk, None if d is None else d * tk)

    def finalize():
        o_ref[...] = _merge_heads(acc_ref).astype(o_ref.dtype)

    _when_tile(qi, kj, tq // tk, update, finalize)


def _fox_prompt(q_all, kvx, c_t, *, tq, tk):
    b, t, _ = q_all.shape
    qi_tab, kj_tab = _tri_tables(t // tq, tq // tk)
    xb = OFF_XB // KVX_AB
    grid_spec = pltpu.PrefetchScalarGridSpec(
        num_scalar_prefetch=2, grid=(b, KV_B, int(qi_tab.shape[0])),
        in_specs=[
            pl.BlockSpec((None, tq, LANES), lambda bi, h, s, qt, kt: (bi, qt[s], KV_A + h)),
            pl.BlockSpec((None, tk, KVX_AB), lambda bi, h, s, qt, kt: (bi, kt[s], xb + h)),
            pl.BlockSpec((None, GROUP, 1, tq), lambda bi, h, s, qt, kt: (bi, h, 0, qt[s])),
            pl.BlockSpec((None, GROUP, 1, tk), lambda bi, h, s, qt, kt: (bi, h, 0, kt[s])),
        ],
        out_specs=pl.BlockSpec((None, tq, LANES), lambda bi, h, s, qt, kt: (bi, qt[s], h)),
        scratch_shapes=[pltpu.VMEM((GROUP, tq, LANES), BF16), pltpu.VMEM((GROUP, tq, LANES), F32),
                        pltpu.VMEM((GROUP, tq, LANES), F32)])
    return pl.pallas_call(
        functools.partial(_fox_kernel, tq=tq, tk=tk), grid_spec=grid_spec,
        out_shape=jax.ShapeDtypeStruct((b, t, W_BRANCH), BF16), compiler_params=_params(3),
        name="fox_prompt")(qi_tab, kj_tab, q_all, kvx, c_t[:, :, None, :], c_t[:, :, None, :])


def _diff_kernel(qi_tab, kj_tab, slope_ref, q_ref, kv_ref, lam_ref, sg_ref, o_ref, qs_ref, m_ref, acc_ref, *, tq, tk,
                 lam_init):
    hk = pl.program_id(1)
    step = pl.program_id(2)
    qi, kj = qi_tab[step], kj_tab[step]

    @pl.when(kj == 0)
    def _():
        _init_flash(m_ref, acc_ref)
        _pad_heads(q_ref, qs_ref, 2 * GROUP)

    def update(d):
        vext = kv_ref[:, 2 * LANES:]
        rel = (kj * tk - qi * tq + _iota((1, tk), 1)).astype(F32)
        for g in range(GROUP):
            bias = slope_ref[hk * GROUP + g] * rel
            for m in range(2):
                i = m * GROUP + g
                logits = lambda r0, r1, c0, c1: (
                    _dot_t(qs_ref[i, r0:r1], kv_ref[c0:c1, m * LANES:(m + 1) * LANES]) + bias[:, c0:c1])
                _tile_update(logits, vext, m_ref, acc_ref, i, tq, tk, None if d is None else d * tk)

    def finalize():
        lam = lam_ref[...]
        lam_val = (jnp.exp(jnp.sum(lam[0:1] * lam[1:2], axis=1, keepdims=True))
                   - jnp.exp(jnp.sum(lam[2:3] * lam[3:4], axis=1, keepdims=True)) + lam_init)
        for g in range(GROUP):
            a0, a1 = acc_ref[g], acc_ref[GROUP + g]
            o = a0[:, :V_DIM_C] / a0[:, V_DIM_C:] - lam_val * (a1[:, :V_DIM_C] / a1[:, V_DIM_C:])
            ms = jnp.mean(o * o, axis=-1, keepdims=True)
            o = (o * lax.rsqrt(ms + RMS_EPS)) * sg_ref[...] * (1.0 - lam_init)
            o_ref[:, g * V_DIM_C:(g + 1) * V_DIM_C] = o.astype(o_ref.dtype)

    _when_tile(qi, kj, tq // tk, update, finalize)


def _diff_prompt(q_all, kvx, slopes, lam_l, sg_l, *, tq, tk, lam_init):
    b, t, _ = q_all.shape
    qi_tab, kj_tab = _tri_tables(t // tq, tq // tk)
    xc = OFF_XC // KVX_C
    grid_spec = pltpu.PrefetchScalarGridSpec(
        num_scalar_prefetch=2, grid=(b, KV_C, int(qi_tab.shape[0])),
        in_specs=[
            pl.BlockSpec(memory_space=pltpu.SMEM),
            pl.BlockSpec((None, tq, 2 * LANES), lambda bi, h, s, qt, kt: (bi, qt[s], 4 + h)),
            pl.BlockSpec((None, tk, KVX_C), lambda bi, h, s, qt, kt: (bi, kt[s], xc + h)),
            pl.BlockSpec((4, HEAD_DIM), lambda bi, h, s, qt, kt: (0, 0)),
            pl.BlockSpec((1, V_DIM_C), lambda bi, h, s, qt, kt: (0, 0)),
        ],
        out_specs=pl.BlockSpec((None, tq, 2 * LANES), lambda bi, h, s, qt, kt: (bi, qt[s], h)),
        scratch_shapes=[pltpu.VMEM((2 * GROUP, tq, LANES), BF16), pltpu.VMEM((2 * GROUP, tq, LANES), F32),
                        pltpu.VMEM((2 * GROUP, tq, 2 * V_DIM_C), F32)])
    return pl.pallas_call(
        functools.partial(_diff_kernel, tq=tq, tk=tk, lam_init=lam_init), grid_spec=grid_spec,
        out_shape=jax.ShapeDtypeStruct((b, t, W_BRANCH), BF16), compiler_params=_params(3),
        name="diff_prompt")(qi_tab, kj_tab, slopes, q_all, kvx, lam_l, sg_l)


def _topk_keep(gate, blk, own, past, nblk, axis):
    gm = jnp.where(past, gate, -jnp.inf)
    cnt = jnp.zeros(gate.shape, jnp.int32)
    for jp in range(nblk):
        gj = gm[jp:jp + 1, :] if axis == 0 else gm[:, jp:jp + 1]
        ahead = (gj > gm) | ((gj == gm) & (jp < blk))
        cnt = cnt + ahead.astype(jnp.int32)
    return (past & (cnt < MOBA_TOPK)) | (blk == own)


def _moba_kernel(qi_tab, kj_tab, slope_ref, q_ref, kv_ref, km_ref, o_ref, qa_ref, m_ref, acc_ref, *, tq, tk, nblk):
    hk = pl.program_id(1)
    step = pl.program_id(2)
    qi, kj = qi_tab[step], kj_tab[step]
    bpt = tk // MOBA_BLOCK
    nbp = -(-nblk // SUBLANES) * SUBLANES
    tile = tq

    @pl.when(kj == 0)
    def _():
        _init_flash(m_ref, acc_ref)
        own = lax.shift_right_logical(qi * tile + _iota((1, tile), 1), _log2(MOBA_BLOCK))
        blk = _iota((nbp, 1), 0)
        zeros = jnp.zeros((tile, HEAD_DIM), BF16)
        for g in range(GROUP):
            qg = q_ref[:, g * HEAD_DIM:(g + 1) * HEAD_DIM]
            gate = _dot_t(km_ref[:nbp, :HEAD_DIM], qg.astype(F32), precision=lax.Precision.HIGHEST)
            keep = _topk_keep(gate, blk, own, blk < own, nblk, 0)
            bias = jnp.where(keep, 0.0, NEG)
            bias = jnp.concatenate([bias, jnp.zeros((LANES - nbp, tile), F32)], axis=0)
            qa_ref[g, :, :LANES] = jnp.concatenate([qg, zeros], axis=1)
            qa_ref[g, :, LANES:] = bias.T.astype(BF16)

    def update(d):
        key_blk = kj * bpt + lax.shift_right_logical(_iota((tk, 1), 0), _log2(MOBA_BLOCK))
        onehot = jnp.where(_iota((1, LANES), 1) == key_blk, 1.0, 0.0).astype(BF16)
        kaug = jnp.concatenate([kv_ref[:, :LANES], onehot], axis=1)
        vext = kv_ref[:, LANES:]
        rel = (kj * tk - qi * tq + _iota((1, tk), 1)).astype(F32)
        for g in range(GROUP):
            bias = slope_ref[hk * GROUP + g] * rel
            logits = lambda r0, r1, c0, c1: _dot_t(qa_ref[g, r0:r1], kaug[c0:c1]) + bias[:, c0:c1]
            _tile_update(logits, vext, m_ref, acc_ref, g, tq, tk, None if d is None else d * tk)

    def finalize():
        o_ref[...] = _merge_heads(acc_ref).astype(o_ref.dtype)

    _when_tile(qi, kj, tq // tk, update, finalize)


def _moba_prompt(q_all, kvx, kmean_pad, slopes, *, tq, tk):
    b, t, _ = q_all.shape
    qi_tab, kj_tab = _tri_tables(t // tq, tq // tk)
    grid_spec = pltpu.PrefetchScalarGridSpec(
        num_scalar_prefetch=2, grid=(b, KV_A, int(qi_tab.shape[0])),
        in_specs=[
            pl.BlockSpec(memory_space=pltpu.SMEM),
            pl.BlockSpec((None, tq, LANES), lambda bi, h, s, qt, kt: (bi, qt[s], h)),
            pl.BlockSpec((None, tk, KVX_AB), lambda bi, h, s, qt, kt: (bi, kt[s], h)),
            pl.BlockSpec((None, LANES, LANES), lambda bi, h, s, qt, kt: (bi, 0, h)),
        ],
        out_specs=pl.BlockSpec((None, tq, LANES), lambda bi, h, s, qt, kt: (bi, qt[s], h)),
        scratch_shapes=[pltpu.VMEM((GROUP, tq, 2 * LANES), BF16), pltpu.VMEM((GROUP, tq, LANES), F32),
                        pltpu.VMEM((GROUP, tq, LANES), F32)])
    return pl.pallas_call(
        functools.partial(_moba_kernel, tq=tq, tk=tk, nblk=t // MOBA_BLOCK), grid_spec=grid_spec,
        out_shape=jax.ShapeDtypeStruct((b, t, W_BRANCH), BF16), compiler_params=_params(3),
        name="moba_prompt")(qi_tab, kj_tab, slopes, q_all, kvx, kmean_pad)


def _rows_of(q_ref, cols):
    return jnp.concatenate([q_ref[:, c:c + HEAD_DIM] for c in cols], axis=0)


def _pad_q(q, front=False):
    zeros = jnp.zeros(q.shape, BF16)
    return jnp.concatenate([zeros, q.astype(BF16)] if front else [q.astype(BF16), zeros], axis=1)


def _col_per_group(vals, n_tok):
    return jnp.concatenate([jnp.full((n_tok, 1), v, F32) for v in vals], axis=0)


def _rows_per_group(x, n_tok):
    return jnp.concatenate([jnp.broadcast_to(x[i:i + 1], (n_tok, x.shape[1])) for i in range(x.shape[0])], axis=0)


def _head_match(rows, cols, nkv, n_tok, col_head):
    row_head = lax.shift_right_logical(_iota((rows, 1), 0), _log2(GROUP * n_tok)) & (nkv - 1)
    return jnp.where(col_head == row_head, 0.0, NEG)


def _new_rows(kvn_ref, nkv):
    w = kvn_ref.shape[1] // nkv
    return jnp.concatenate([kvn_ref[:, h * w:(h + 1) * w] for h in range(nkv)], axis=0)


def _new_bias(rows, nkv, n_tok):
    col = _iota((1, nkv * n_tok), 1)
    bias = _head_match(rows, nkv * n_tok, nkv, n_tok, lax.shift_right_logical(col, _log2(n_tok)))
    visible = (col & (n_tok - 1)) <= (_iota((rows, 1), 0) & (n_tok - 1))
    return jnp.where(visible, bias, NEG), (col & (n_tok - 1)).astype(F32)


def _multi_update(s_list, v_list, m_ref, l_ref, acc_ref, shift=0):
    idx = ...
    m_prev = m_ref[idx]
    smax = s_list[0]
    for s in s_list[1:]:
        smax = jnp.maximum(smax, s)
    m_next = jnp.maximum(m_prev, jnp.max(smax, axis=1, keepdims=True))
    alpha = jnp.exp2(m_prev - m_next)
    psum, pv = None, None
    for s, v in zip(s_list, v_list):
        p = jnp.exp2(s - m_next)
        pm = pltpu.roll(p, shift, axis=1) if shift else p
        d = jnp.dot(pm.astype(BF16), v, preferred_element_type=F32)
        psum = p if psum is None else psum + p
        pv = d if pv is None else pv + d
    l_ref[idx] = alpha * l_ref[idx] + jnp.sum(psum, axis=1, keepdims=True)
    acc_ref[idx] = alpha * acc_ref[idx] + pv
    m_ref[idx] = m_next


def _init_multi(m_ref, l_ref, acc_ref):
    m_ref[...] = jnp.full(m_ref.shape, NEG, F32)
    l_ref[...] = jnp.zeros(l_ref.shape, F32)
    acc_ref[...] = jnp.zeros(acc_ref.shape, F32)


def _fox_sample_kernel(pt_ref, q_ref, kvn_ref, lfn_ref, *rest, npg, n_tok):
    pages, lfs = rest[:npg], rest[npg:2 * npg]
    o_ref, m_ref, l_ref, acc_ref, carry_ref = rest[2 * npg:]
    j = pl.program_id(1)
    rows, pr = H_B * n_tok, PAGE_SIZE * KV_B

    @pl.when(j == 0)
    def _():
        _init_multi(m_ref, l_ref, acc_ref)
        carry_ref[...] = jnp.zeros(carry_ref.shape, F32)

    q = _pad_q(_rows_of(q_ref, [h * HEAD_DIM for h in range(H_B)]))
    col = _iota((1, pr), 1)
    match = _head_match(rows, pr, KV_B, n_tok, col & (KV_B - 1))
    upper = (_iota((PAGE_SIZE, 1), 0) <= lax.shift_right_logical(col, _log2(KV_B))).astype(F32)
    carry = carry_ref[...]
    s_list, v_list = [], []
    c_local = jnp.dot(jnp.concatenate([lf[...] for lf in lfs], axis=0), upper, preferred_element_type=F32,
                      precision=lax.Precision.HIGHEST)
    for r in range(npg):
        c_full = c_local[r * H_B:(r + 1) * H_B] + carry
        carry = c_full[:, pr - 1:pr]
        bias = match - _rows_per_group(c_full, n_tok) * LOG2E
        v_list.append(pages[r][...].astype(BF16))
        s_list.append(_dot_t(q, v_list[-1]) + bias)
    carry_ref[...] = carry
    _multi_update(s_list, v_list, m_ref, l_ref, acc_ref)

    @pl.when(j == pl.num_programs(1) - 1)
    def _():
        kvn = _new_rows(kvn_ref, KV_B).astype(BF16)
        bias, _ = _new_bias(rows, KV_B, n_tok)
        tok = _iota((1, KV_B * n_tok), 1) & (n_tok - 1)
        upper_n = (_iota((n_tok, 1), 0) <= tok).astype(F32)
        c_new = _dot_tl(lfn_ref[...], upper_n) + carry
        s = _dot_t(q, kvn) + (bias - _rows_per_group(c_new, n_tok) * LOG2E)
        _multi_update([s], [kvn], m_ref, l_ref, acc_ref)
        o = acc_ref[:, HEAD_DIM:] / l_ref[...]
        for h in range(H_B):
            o_ref[:, h * HEAD_DIM:(h + 1) * HEAD_DIM] = o[h * n_tok:(h + 1) * n_tok]


def _diff_sample_kernel(pt_ref, slope_ref, lam_ref, sg_ref, q_ref, kvn_ref, *rest, npg, n_tok, lam_init):
    pages = rest[:npg]
    o_ref, m_ref, l_ref, acc_ref = rest[npg:]
    j = pl.program_id(1)
    half = H_C * n_tok
    rows, pr = 2 * half, 2 * PAGE_SIZE * KV_C
    n_past = pl.num_programs(1) * npg * PAGE_SIZE

    @pl.when(j == 0)
    def _():
        _init_multi(m_ref, l_ref, acc_ref)

    qs = [_rows_of(q_ref, [((hk * 2 + m) * GROUP + g) * HEAD_DIM for hk in range(KV_C) for g in range(GROUP)])
          for m in range(2)]
    qpad = [_pad_q(qs[0]), _pad_q(qs[1], front=True)]
    slope = _col_per_group([slope_ref[h] for h in range(H_C)] * 2, n_tok)

    def logits(kv, bias):
        return jnp.concatenate([_dot_t(qs[m].astype(BF16), kv[:, m * HEAD_DIM:(m + 1) * HEAD_DIM].astype(BF16))
                                for m in range(2)], axis=0) + bias

    col = _iota((1, pr), 1)
    key_head = jnp.where((col & KV_C) == 0, col & (KV_C - 1), -1)
    match = _head_match(rows, pr, KV_C, n_tok, key_head)
    tok = lax.shift_right_logical(col, _log2(2 * KV_C))
    s_list, v_list = [], []
    for r in range(npg):
        rel = ((j * npg + r) * PAGE_SIZE - n_past + tok).astype(F32)
        v_list.append(pages[r][...].astype(BF16))
        s_list.append(jnp.concatenate([_dot_t(qp, v_list[-1]) for qp in qpad], axis=0) + (match + slope * rel))
    _multi_update(s_list, v_list, m_ref, l_ref, acc_ref, shift=KV_C)

    @pl.when(j == pl.num_programs(1) - 1)
    def _():
        kvn = _new_rows(kvn_ref, KV_C)
        bias, rel = _new_bias(rows, KV_C, n_tok)
        _multi_update([logits(kvn, bias + slope * rel)], [kvn[:, 2 * HEAD_DIM:].astype(BF16)], m_ref, l_ref, acc_ref)
        lam = lam_ref[...]
        lam_val = (jnp.exp(jnp.sum(lam[0:1] * lam[1:2], axis=1, keepdims=True))
                   - jnp.exp(jnp.sum(lam[2:3] * lam[3:4], axis=1, keepdims=True)) + lam_init)
        on = acc_ref[...] / l_ref[...]
        o = on[:half] - lam_val * on[half:]
        ms = jnp.mean(o * o, axis=-1, keepdims=True)
        o = (o * lax.rsqrt(ms + RMS_EPS)) * sg_ref[...] * (1.0 - lam_init)
        for h in range(H_C):
            o_ref[:, h * V_DIM_C:(h + 1) * V_DIM_C] = o[h * n_tok:(h + 1) * n_tok]


def _moba_sample_kernel(pt_ref, slope_ref, q_ref, kvn_ref, *rest, npg, n_tok, nblk):
    pages = rest[:npg]
    o_ref, mb_ref, lb_ref, gb_ref, ob_ref = rest[npg:]
    j = pl.program_id(1)
    rows, pr = H_A * n_tok, PAGE_SIZE * KV_A
    ppb = MOBA_BLOCK // PAGE_SIZE
    bps = npg // ppb
    lane = _iota((1, LANES), 1)
    n_past = pl.num_programs(1) * npg * PAGE_SIZE

    @pl.when(j == 0)
    def _():
        mb_ref[...] = jnp.full(mb_ref.shape, NEG, F32)
        lb_ref[...] = jnp.zeros(lb_ref.shape, F32)
        gb_ref[...] = jnp.zeros(gb_ref.shape, F32)

    qf = _rows_of(q_ref, [h * HEAD_DIM for h in range(H_A)])
    q = _pad_q(qf)
    slope = _col_per_group([slope_ref[h] for h in range(H_A)], n_tok)
    col = _iota((1, pr), 1)
    match = _head_match(rows, pr, KV_A, n_tok, col & (KV_A - 1))
    tok = lax.shift_right_logical(col, _log2(KV_A))
    row_kv = lax.shift_right_logical(_iota((rows, 1), 0), _log2(GROUP * n_tok))

    for bl in range(bps):
        blk = j * bps + bl
        s_l, v_l, kmean = [], [], None
        for r in range(ppb):
            page = pages[bl * ppb + r]
            rel = ((blk * ppb + r) * PAGE_SIZE - n_past + tok).astype(F32)
            v_l.append(page[...].astype(BF16))
            s_l.append(_dot_t(q, v_l[-1]) + (match + slope * rel))
            km = jnp.sum(page[:, :HEAD_DIM].reshape(pr // SUBLANES, SUBLANES, HEAD_DIM), axis=0)
            kmean = km if kmean is None else kmean + km
        mb = jnp.max(functools.reduce(jnp.maximum, s_l), axis=1, keepdims=True)
        ps = [jnp.exp2(s - mb) for s in s_l]
        lb = jnp.sum(functools.reduce(jnp.add, ps), axis=1, keepdims=True)
        ob_ref[blk] = functools.reduce(
            jnp.add, [jnp.dot(p.astype(BF16), v, preferred_element_type=F32) for p, v in zip(ps, v_l)])
        kmean = (kmean[:KV_A] + kmean[KV_A:]) * (1.0 / MOBA_BLOCK)
        kmean = jnp.concatenate([kmean, jnp.zeros((SUBLANES - KV_A, HEAD_DIM), F32)], axis=0)
        gate_all = _dot_t(qf, kmean, precision=lax.Precision.HIGHEST)
        gate = jnp.sum(jnp.where(_iota((1, SUBLANES), 1) == row_kv, gate_all, 0.0), axis=1, keepdims=True)
        here = lane == blk
        mb_ref[...] = jnp.where(here, mb, mb_ref[...])
        lb_ref[...] = jnp.where(here, lb, lb_ref[...])
        gb_ref[...] = jnp.where(here, gate, gb_ref[...])

    @pl.when(j == pl.num_programs(1) - 1)
    def _():
        keep = _topk_keep(gb_ref[...], lane, nblk, lane < nblk, nblk, 1) & (lane < nblk)
        m_sel = jnp.where(keep, mb_ref[...], NEG)
        kvn = _new_rows(kvn_ref, KV_A).astype(BF16)
        bias, rel = _new_bias(rows, KV_A, n_tok)
        s = _dot_t(q, kvn) + (bias + slope * rel)
        m_tot = jnp.maximum(jnp.max(s, axis=1, keepdims=True), jnp.max(m_sel, axis=1, keepdims=True))
        p = jnp.exp2(s - m_tot)
        w = jnp.where(keep, jnp.exp2(m_sel - m_tot), 0.0)
        den = jnp.sum(p, axis=1, keepdims=True) + jnp.sum(w * lb_ref[...], axis=1, keepdims=True)
        num = jnp.dot(p.astype(BF16), kvn, preferred_element_type=F32)
        for b in range(nblk):
            num = num + w[:, b:b + 1] * ob_ref[b]
        o = num[:, HEAD_DIM:] / den
        for h in range(H_A):
            o_ref[:, h * HEAD_DIM:(h + 1) * HEAD_DIM] = o[h * n_tok:(h + 1) * n_tok]


def _page_specs(layer, n_pages, npg, page_shape):
    zeros = (0,) * len(page_shape)

    def spec(r):
        return pl.BlockSpec((None, None) + tuple(page_shape),
                            lambda b, j, pt: (layer, pt[b * n_pages + j * npg + r]) + zeros)
    return [spec(r) for r in range(npg)]


def _sample_attention(kind, layer, pt_flat, q_s, kvn, cache, *, n_pages, extra=(), extra_specs=(), logf_cache=None,
                      lfn=None, lam_init=0.0):
    db, n_tok, _ = q_s.shape
    npg = PAGES_PER_STEP if n_pages % PAGES_PER_STEP == 0 else MOBA_BLOCK // PAGE_SIZE
    per_b = lambda w: pl.BlockSpec((None, n_tok, w), lambda b, j, pt: (b, 0, 0))
    qcol = {"moba": 0, "fox": 1, "diff": 2}[kind]
    q_spec = pl.BlockSpec((None, n_tok, W_BRANCH), lambda b, j, pt: (b, 0, qcol))
    in_specs = list(extra_specs) + [q_spec, per_b(W_BRANCH)]
    args = list(extra) + [q_s, kvn]
    if kind == "fox":
        in_specs.append(per_b(H_B))
        args.append(lfn)
    in_specs += _page_specs(layer, n_pages, npg, cache.shape[2:])
    args += [cache] * npg
    col = lambda r, w: pltpu.VMEM((r, w), F32)
    if kind == "fox":
        rows = H_B * n_tok
        in_specs += _page_specs(layer, n_pages, npg, (H_B, PAGE_SIZE))
        args += [logf_cache] * npg
        kern = functools.partial(_fox_sample_kernel, npg=npg, n_tok=n_tok)
        scratch = [col(rows, 1), col(rows, 1), col(rows, 2 * HEAD_DIM), col(H_B, 1)]
    elif kind == "diff":
        rows = 2 * H_C * n_tok
        kern = functools.partial(_diff_sample_kernel, npg=npg, n_tok=n_tok, lam_init=lam_init)
        scratch = [col(rows, 1), col(rows, 1), col(rows, V_DIM_C)]
    else:
        rows = H_A * n_tok
        nblk = n_pages * PAGE_SIZE // MOBA_BLOCK
        kern = functools.partial(_moba_sample_kernel, npg=npg, n_tok=n_tok, nblk=nblk)
        scratch = [col(rows, LANES), col(rows, LANES), col(rows, LANES), pltpu.VMEM((nblk, rows, 2 * HEAD_DIM), F32)]
    grid_spec = pltpu.PrefetchScalarGridSpec(
        num_scalar_prefetch=1, grid=(db, n_pages // npg), in_specs=in_specs, out_specs=per_b(W_BRANCH),
        scratch_shapes=scratch)
    return pl.pallas_call(kern, grid_spec=grid_spec, out_shape=jax.ShapeDtypeStruct((db, n_tok, W_BRANCH), F32),
                          compiler_params=_params(2), name=kind + "_sample")(pt_flat, *args)


def _alibi_slopes(n):
    return 2.0 ** (-8.0 * jnp.arange(1, n + 1, dtype=F32) / n)


def kernel(x_prompt, x_sample, cache_a_kv, cache_b_kv, cache_b_logf, cache_c_kv, page_table, w_in, w_o_a, w_o_b,
           w_o_c, w_out, norm_g, forget_b, diff_lambda, diff_subln_g, final_norm_g):
    depth = w_in.shape[0]
    bp, t, _ = x_prompt.shape
    db, n_tok, _ = x_sample.shape
    n_pages = page_table.shape[1]
    n_pool = cache_a_kv.shape[1]
    assert t % MOBA_BLOCK == 0 and (n_pages * PAGE_SIZE) % MOBA_BLOCK == 0
    assert t // MOBA_BLOCK <= LANES and n_pages * PAGE_SIZE // MOBA_BLOCK <= LANES
    assert n_tok == SUBLANES

    w_perm = _permute_columns(w_in)
    woa, wob, woc, wout = (w.astype(BF16) for w in (w_o_a, w_o_b, w_o_c, w_out))
    norm_g3 = norm_g.reshape(depth, 1, D_MODEL)
    fb3 = jnp.pad(forget_b, ((0, 0), (0, FB_PAD - H_B))).reshape(depth, 1, FB_PAD)
    fg2 = final_norm_g.reshape(1, D_MODEL)
    slopes_a, slopes_c = _alibi_slopes(H_A) * LOG2E, _alibi_slopes(H_C) * LOG2E
    pt_flat = page_table.reshape(-1).astype(jnp.int32)
    ca = cache_a_kv.reshape(depth, n_pool, PAGE_SIZE * KV_A, 2 * HEAD_DIM)
    cb = cache_b_kv.reshape(depth, n_pool, PAGE_SIZE * KV_B, 2 * HEAD_DIM)
    logf_t = jnp.swapaxes(cache_b_logf, 2, 3)
    cc = cache_c_kv.reshape(depth, n_pool, PAGE_SIZE, KV_C, 2, LANES).transpose(0, 1, 2, 4, 3, 5)
    cc = cc.reshape(depth, n_pool, 2 * PAGE_SIZE * KV_C, LANES)

    tm_p = 256 if (bp * t) % 256 == 0 else bp * t
    tk = next((c for c in (1024, 512) if t % c == 0), t)
    tq = Q_TILE_MULT * tk if t % (Q_TILE_MULT * tk) == 0 else tk
    ns = db * n_tok
    smem = pl.BlockSpec(memory_space=pltpu.SMEM)

    xp = x_prompt.reshape(bp * t, D_MODEL)
    xs = x_sample.reshape(ns, D_MODEL)
    outs_p = [[], [], [], []]
    outs_s = [[], [], [], []]
    for l in range(depth):
        lam_init = 0.8 - 0.6 * math.exp(-0.3 * l)
        sg = diff_subln_g[l].reshape(1, V_DIM_C)
        final = l == depth - 1

        kva, kvb, kvc, logf, kvx, q_all, gates, gm = _inproj(xp, l, norm_g3, fb3, w_perm, tm=tm_p, q_dtype=BF16)
        for acc, v in zip(outs_p, (kva, kvb, logf, kvc)):
            acc.append(v)
        q3 = q_all.reshape(bp, t, 3 * W_BRANCH)
        kv3 = kvx.reshape(bp, t, W_KVX)
        km = _kmean(kva).reshape(bp, t // MOBA_BLOCK, W_BRANCH)
        km = jnp.pad(km, ((0, 0), (0, LANES - t // MOBA_BLOCK), (0, 0)))
        c_t = _cumsum_t(jnp.swapaxes(logf.reshape(bp, t, H_B), 1, 2))
        oa = _moba_prompt(q3, kv3, km, slopes_a, tq=tq, tk=tk)
        ob = _fox_prompt(q3, kv3, c_t, tq=tq, tk=tk)
        oc = _diff_prompt(q3, kv3, slopes_c, diff_lambda[l], sg, tq=tq, tk=tk, lam_init=lam_init)
        xp = _outproj(xp, oa.reshape(bp * t, W_BRANCH), ob.reshape(bp * t, W_BRANCH), oc.reshape(bp * t, W_BRANCH),
                      gates, gm, l, woa, wob, woc, wout, fg2, tm=tm_p, final=final)

        kva, kvb, kvc, logf, _, q_all, gates, gm = _inproj(xs, l, norm_g3, fb3, w_perm, tm=ns, q_dtype=F32)
        for acc, v in zip(outs_s, (kva, kvb, logf, kvc)):
            acc.append(v)
        q3 = q_all.reshape(db, n_tok, 3 * W_BRANCH)
        r3 = lambda a: a.reshape(db, n_tok, a.shape[-1])
        oa = _sample_attention("moba", l, pt_flat, q3, r3(kva), ca, n_pages=n_pages, extra=(slopes_a,),
                               extra_specs=(smem,))
        ob = _sample_attention("fox", l, pt_flat, q3, r3(kvb), cb, n_pages=n_pages, logf_cache=logf_t,
                               lfn=r3(logf))
        oc = _sample_attention("diff", l, pt_flat, q3, r3(kvc), cc, n_pages=n_pages,
                               extra=(slopes_c, diff_lambda[l], sg),
                               extra_specs=(smem, pl.BlockSpec((4, HEAD_DIM), lambda b, j, pt: (0, 0)),
                                            pl.BlockSpec((1, V_DIM_C), lambda b, j, pt: (0, 0))),
                               lam_init=lam_init)
        xs = _outproj(xs, oa.reshape(ns, W_BRANCH), ob.reshape(ns, W_BRANCH), oc.reshape(ns, W_BRANCH), gates, gm, l,
                      woa, wob, woc, wout, fg2, tm=ns, final=final)

    def stack(vals, b_, t_, tail):
        return jnp.stack(vals).reshape((depth, b_, t_) + tail)

    pa, pb, pf, pc = outs_p
    sa, sb, sf, sc = outs_s
    kv_tail, c_tail = (KV_A, 2 * HEAD_DIM), (KV_C, 2 * HEAD_DIM + V_DIM_C)
    return (xp.reshape(bp, t, D_MODEL), xs.reshape(db, n_tok, D_MODEL),
            stack(pa, bp, t, kv_tail), stack(pb, bp, t, kv_tail), stack(pf, bp, t, (H_B,)), stack(pc, bp, t, c_tail),
            stack(sa, db, n_tok, kv_tail), stack(sb, db, n_tok, kv_tail), stack(sf, db, n_tok, (H_B,)),
            stack(sc, db, n_tok, c_tail))
```

```python
import functools
import math

import jax
import jax.numpy as jnp
import numpy as np
from jax import lax
from jax.experimental import pallas as pl
from jax.experimental.pallas import tpu as pltpu

F32 = jnp.float32
BF16 = jnp.bfloat16

D_MODEL = 1024
HEAD_DIM = 64
PAGE_SIZE = 128
H_A, KV_A = 8, 4
H_B, KV_B = 8, 4
H_C, KV_C = 4, 2
GROUP = 2
V_DIM_C = 2 * HEAD_DIM
MOBA_BLOCK = 256
MOBA_TOPK = 3
N_BRANCH = 3
RMS_EPS = 1e-6
W_BRANCH = 512
IN_NAMES = ("qa", "ka", "va", "ga", "qb", "kb", "vb", "fb", "gb", "qc", "kc", "vc", "gc", "gm")
IN_SIZES = (512, 256, 256, 512, 512, 256, 256, 8, 512, 512, 256, 256, 512, N_BRANCH * D_MODEL)
D_IN = sum(IN_SIZES)

NEG = -1e30
LOG2E = math.log2(math.e)
LANES = 128
SUBLANES = 8
FB_PAD = LANES
OFF_KVA, OFF_KVB, OFF_KVC, OFF_FB = 0, 512, 1024, 1536
OFF_Q = OFF_FB + FB_PAD
OFF_G = OFF_Q + 3 * W_BRANCH
OFF_GM = OFF_G + 3 * W_BRANCH
D_PERM = OFF_GM + N_BRANCH * D_MODEL
KVX_AB = 4 * HEAD_DIM
KVX_C = 8 * HEAD_DIM
OFF_XA, OFF_XB, OFF_XC = 0, KV_A * KVX_AB, KV_A * KVX_AB + KV_B * KVX_AB
W_KVX = OFF_XC + KV_C * KVX_C
VMEM_LIMIT = 56 * 1024 * 1024
Q_TILE_MULT = 2
PAGES_PER_STEP = 32


def _perm_and_scale():
    off = dict(zip(IN_NAMES, np.concatenate([[0], np.cumsum(IN_SIZES)[:-1]])))
    d = HEAD_DIM
    cols = []
    for kn, vn, nkv in (("ka", "va", KV_A), ("kb", "vb", KV_B)):
        for h in range(nkv):
            cols += list(range(off[kn] + h * d, off[kn] + (h + 1) * d))
            cols += list(range(off[vn] + h * d, off[vn] + (h + 1) * d))
    for h in range(KV_C):
        for m in range(2):
            cols += list(range(off["kc"] + (m * KV_C + h) * d, off["kc"] + (m * KV_C + h + 1) * d))
        cols += list(range(off["vc"] + h * V_DIM_C, off["vc"] + (h + 1) * V_DIM_C))
    cols += list(range(off["fb"], off["fb"] + H_B)) + [D_IN] * (FB_PAD - H_B)
    cols += list(range(off["qa"], off["qa"] + 512))
    cols += list(range(off["qb"], off["qb"] + 512))
    for h in range(KV_C):
        for m in range(2):
            for g in range(GROUP):
                s = off["qc"] + ((m * KV_C + h) * GROUP + g) * d
                cols += list(range(s, s + d))
    for n in ("ga", "gb", "gc"):
        cols += list(range(off[n], off[n] + 512))
    cols += list(range(off["gm"], off["gm"] + N_BRANCH * D_MODEL))
    perm = np.asarray(cols, np.int32)
    assert perm.shape[0] == D_PERM
    scale = np.ones((D_PERM,), np.float32)
    scale[OFF_Q:OFF_G] = HEAD_DIM ** -0.5 * LOG2E
    return perm, scale


_PERM, _SCALE = _perm_and_scale()


def _permute_columns(w_in):
    pieces, start = [], 0
    for i in range(1, D_PERM + 1):
        joined = i < D_PERM and _SCALE[i] == _SCALE[start] and (
            _PERM[i] == _PERM[i - 1] + 1 or _PERM[i] == _PERM[i - 1] == D_IN)
        if not joined:
            a, n = int(_PERM[start]), i - start
            if a == D_IN:
                pieces.append(jnp.zeros(w_in.shape[:2] + (n,), BF16))
            else:
                pieces.append((w_in[:, :, a:a + n] * float(_SCALE[start])).astype(BF16))
            start = i
    return jnp.concatenate(pieces, axis=2)


def _params(n_axes):
    return pltpu.CompilerParams(dimension_semantics=("arbitrary",) * n_axes, vmem_limit_bytes=VMEM_LIMIT)


def _sigmoid(x):
    return 1.0 / (1.0 + jnp.exp(-x))


def _dot_t(a, b, precision=None):
    return lax.dot_general(a, b, (((1,), (1,)), ((), ())), preferred_element_type=F32, precision=precision)


def _dot_tl(a, b):
    return lax.dot_general(a, b, (((0,), (0,)), ((), ())), preferred_element_type=F32,
                           precision=lax.Precision.HIGHEST)


def _iota(shape, axis):
    return lax.broadcasted_iota(jnp.int32, shape, axis)


def _log2(n):
    assert n & (n - 1) == 0
    return n.bit_length() - 1


def _store_heads(ref, z):
    if len(ref.shape) == 2:
        ref[...] = z
        return
    w = ref.shape[2]
    for h in range(ref.shape[1]):
        ref[:, h, :] = z[:, h * w:(h + 1) * w]


def _inproj_kernel(x_ref, g_ref, fb_ref, w_ref, kva_ref, kvb_ref, kvc_ref, logf_ref, kvx_ref, q_ref, gate_ref,
                   gm_ref):
    x = x_ref[...]
    ms = jnp.mean(x * x, axis=-1, keepdims=True)
    h = ((x * lax.rsqrt(ms + RMS_EPS)) * g_ref[...]).astype(BF16)

    def mm(a, n):
        return jnp.dot(h, w_ref[:, a:a + n], preferred_element_type=F32)

    low = (_iota((1, W_BRANCH), 1) & (LANES - 1)) < HEAD_DIM
    for i, (ref, off_x) in enumerate(((kva_ref, OFF_XA), (kvb_ref, OFF_XB))):
        z = mm(i * W_BRANCH, W_BRANCH)
        _store_heads(ref, z)
        kx = jnp.where(low, z, 0.0).astype(BF16)
        vx = jnp.where(low, pltpu.roll(z, W_BRANCH - HEAD_DIM, axis=1), 1.0).astype(BF16)
        for hk in range(KV_A):
            kvx_ref[:, off_x + hk * KVX_AB: off_x + hk * KVX_AB + LANES] = kx[:, hk * LANES:(hk + 1) * LANES]
            kvx_ref[:, off_x + hk * KVX_AB + LANES: off_x + (hk + 1) * KVX_AB] = vx[:, hk * LANES:(hk + 1) * LANES]
    z = mm(OFF_KVC, W_BRANCH)
    _store_heads(kvc_ref, z)
    k1 = jnp.where(low, z, 0.0).astype(BF16)
    k2 = jnp.where(low, pltpu.roll(z, W_BRANCH - HEAD_DIM, axis=1), 0.0).astype(BF16)
    for hk in range(KV_C):
        c, o = hk * 2 * LANES, OFF_XC + hk * KVX_C
        kvx_ref[:, o: o + LANES] = k1[:, c:c + LANES]
        kvx_ref[:, o + LANES: o + 2 * LANES] = k2[:, c:c + LANES]
        kvx_ref[:, o + 2 * LANES: o + 3 * LANES] = z[:, c + LANES: c + 2 * LANES].astype(BF16)
        kvx_ref[:, o + 3 * LANES: o + 4 * LANES] = jnp.ones((z.shape[0], LANES), BF16)
    zf = mm(OFF_FB, FB_PAD) + fb_ref[...]
    lf = jnp.minimum(zf, 0.0) - jnp.log(1.0 + jnp.exp(-jnp.abs(zf)))
    logf_ref[...] = lf[:, :H_B]
    for i in range(3):
        q_ref[:, i * W_BRANCH:(i + 1) * W_BRANCH] = mm(OFF_Q + i * W_BRANCH, W_BRANCH).astype(q_ref.dtype)
        gate_ref[:, i * W_BRANCH:(i + 1) * W_BRANCH] = mm(OFF_G + i * W_BRANCH, W_BRANCH).astype(BF16)
    for i in range(N_BRANCH * D_MODEL // W_BRANCH):
        gm_ref[:, i * W_BRANCH:(i + 1) * W_BRANCH] = mm(OFF_GM + i * W_BRANCH, W_BRANCH).astype(BF16)


def _inproj(x2d, layer, norm_g3, fb3, w_perm, *, tm, q_dtype, split_heads):
    n = x2d.shape[0]
    row = lambda w: pl.BlockSpec((tm, w), lambda i: (i, 0))
    lay = lambda shp: pl.BlockSpec((None,) + shp, lambda i: (layer, 0, 0))
    kv_shapes = [(KV_A, 2 * HEAD_DIM), (KV_B, 2 * HEAD_DIM), (KV_C, 2 * HEAD_DIM + V_DIM_C)]
    if split_heads:
        kv_out = [jax.ShapeDtypeStruct((n,) + shp, F32) for shp in kv_shapes]
        kv_specs = [pl.BlockSpec((tm,) + shp, lambda i: (i, 0, 0)) for shp in kv_shapes]
    else:
        kv_out = [jax.ShapeDtypeStruct((n, W_BRANCH), F32)] * 3
        kv_specs = [row(W_BRANCH)] * 3
    out_shape = (
        *kv_out, jax.ShapeDtypeStruct((n, H_B), F32),
        jax.ShapeDtypeStruct((n, W_KVX), BF16), jax.ShapeDtypeStruct((n, 3 * W_BRANCH), q_dtype),
        jax.ShapeDtypeStruct((n, 3 * W_BRANCH), BF16), jax.ShapeDtypeStruct((n, N_BRANCH * D_MODEL), BF16))
    return pl.pallas_call(
        _inproj_kernel, grid=(n // tm,),
        in_specs=[row(D_MODEL), lay((1, D_MODEL)), lay((1, FB_PAD)), lay((D_MODEL, D_PERM))],
        out_specs=[*kv_specs, row(H_B), row(W_KVX), row(3 * W_BRANCH), row(3 * W_BRANCH), row(N_BRANCH * D_MODEL)],
        out_shape=out_shape, compiler_params=_params(1), name="inproj")(x2d, norm_g3, fb3, w_perm)


def _outproj_kernel(x_ref, oa_ref, ob_ref, oc_ref, gate_ref, gm_ref, woa_ref, wob_ref, woc_ref, wout_ref, fg_ref,
                    y_ref, *, final):
    def branch(i, o_ref, w_ref):
        g = gate_ref[:, i * W_BRANCH:(i + 1) * W_BRANCH].astype(F32)
        u = (o_ref[...].astype(F32) * (g * _sigmoid(g))).astype(BF16)
        p = jnp.dot(u, w_ref[...], preferred_element_type=F32)
        return _sigmoid(gm_ref[:, i * D_MODEL:(i + 1) * D_MODEL].astype(F32)) * p

    merged = branch(0, oa_ref, woa_ref) + branch(1, ob_ref, wob_ref) + branch(2, oc_ref, woc_ref)
    y = x_ref[...] + jnp.dot(merged.astype(BF16), wout_ref[...], preferred_element_type=F32)
    if final:
        ms = jnp.mean(y * y, axis=-1, keepdims=True)
        y = (y * lax.rsqrt(ms + RMS_EPS)) * fg_ref[...]
    y_ref[...] = y


def _outproj(x2d, oa, ob, oc, gates, gm, layer, woa, wob, woc, wout, fg2, *, tm, final):
    n = x2d.shape[0]
    row = lambda w: pl.BlockSpec((tm, w), lambda i: (i, 0))
    lay = lambda shp: pl.BlockSpec((None,) + shp, lambda i: (layer, 0, 0))
    return pl.pallas_call(
        functools.partial(_outproj_kernel, final=final), grid=(n // tm,),
        in_specs=[row(D_MODEL), row(W_BRANCH), row(W_BRANCH), row(W_BRANCH), row(3 * W_BRANCH),
                  row(N_BRANCH * D_MODEL), lay((W_BRANCH, D_MODEL)), lay((W_BRANCH, D_MODEL)),
                  lay((W_BRANCH, D_MODEL)), lay((D_MODEL, D_MODEL)), pl.BlockSpec((1, D_MODEL), lambda i: (0, 0))],
        out_specs=row(D_MODEL), out_shape=jax.ShapeDtypeStruct((n, D_MODEL), F32),
        compiler_params=_params(1), name="outproj")(x2d, oa, ob, oc, gates, gm, woa, wob, woc, wout, fg2)


def _kmean_kernel(kv_ref, o_ref, *, nb):
    for i in range(nb):
        blk = kv_ref[i * MOBA_BLOCK:(i + 1) * MOBA_BLOCK, :].astype(F32)
        o_ref[i:i + 1, :] = jnp.mean(blk, axis=0, keepdims=True)


def _kmean(kvx2d):
    nblk = kvx2d.shape[0] // MOBA_BLOCK
    nb = SUBLANES if nblk % SUBLANES == 0 else nblk
    w = KV_A * KVX_AB
    return pl.pallas_call(
        functools.partial(_kmean_kernel, nb=nb), grid=(nblk // nb,),
        in_specs=[pl.BlockSpec((nb * MOBA_BLOCK, w), lambda i: (i, OFF_XA // w))],
        out_specs=pl.BlockSpec((nb, w), lambda i: (i, 0)),
        out_shape=jax.ShapeDtypeStruct((nblk, w), F32), compiler_params=_params(1), name="kmean")(kvx2d)


def _cumsum_kernel(x_ref, o_ref, *, chunk):
    t = x_ref.shape[1]
    u = (_iota((chunk, chunk), 0) <= _iota((chunk, chunk), 1)).astype(F32)

    def body(i, carry):
        st = pl.multiple_of(i * chunk, chunk)
        seg = x_ref[:, pl.ds(st, chunk)]
        cs = jnp.dot(seg, u, preferred_element_type=F32, precision=lax.Precision.HIGHEST) + carry
        o_ref[:, pl.ds(st, chunk)] = cs
        return cs[:, chunk - 1:chunk]

    lax.fori_loop(0, t // chunk, body, jnp.zeros((x_ref.shape[0], 1), F32))


def _cumsum_t(xt):
    b, h, t = xt.shape
    chunk = 256 if t % 256 == 0 else t
    spec = pl.BlockSpec((None, h, t), lambda i: (i, 0, 0))
    return pl.pallas_call(functools.partial(_cumsum_kernel, chunk=chunk), grid=(b,), in_specs=[spec], out_specs=spec,
                          out_shape=jax.ShapeDtypeStruct(xt.shape, F32), compiler_params=_params(1),
                          name="cumsum")(xt)


def _tri_tables(nq, kpq):
    qs, ks = [], []
    for qi in range(nq):
        for kj in range((qi + 1) * kpq):
            qs.append(qi)
            ks.append(kj)
    return jnp.asarray(np.asarray(qs, np.int32)), jnp.asarray(np.asarray(ks, np.int32))


def _flash_update(s, vext, m_ref, acc_ref, idx, r0=0):
    rows = slice(r0, r0 + s.shape[0])
    m_prev = m_ref[idx, rows]
    m_next = jnp.maximum(m_prev, jnp.max(s, axis=1, keepdims=True))
    alpha = jnp.exp2(m_prev - m_next)
    p = jnp.exp2(s - jnp.tile(m_next, (1, s.shape[1] // LANES)))
    pv = jnp.dot(p.astype(BF16), vext, preferred_element_type=F32)
    acc_ref[idx, rows] = jnp.tile(alpha, (1, pv.shape[1] // LANES)) * acc_ref[idx, rows] + pv
    m_ref[idx, rows] = m_next


def _init_flash(m_ref, acc_ref):
    m_ref[...] = jnp.full(m_ref.shape, NEG, F32)
    acc_ref[...] = jnp.zeros(acc_ref.shape, F32)


def _tile_update(logits, vext, m_ref, acc_ref, idx, tq, tk, diag):
    if diag is None:
        _flash_update(logits(0, tq, 0, tk), vext, m_ref, acc_ref, idx)
        return
    h = tk // 2
    for r0, c0, c1 in ((diag, 0, h), (diag + h, h, tk)):
        s = logits(r0, tq, c0, c1)
        s = jnp.where(_iota(s.shape, 1) <= _iota(s.shape, 0), s, NEG)
        _flash_update(s, vext[c0:c1], m_ref, acc_ref, idx, r0=r0)


def _when_tile(qi, kj, kpq, update, finalize):
    @pl.when(kj < qi * kpq)
    def _():
        update(None)

    for d in range(kpq):
        @pl.when(kj == qi * kpq + d)
        def _():
            update(d)
            if d == kpq - 1:
                finalize()


def _pad_heads(q_ref, qs_ref, n):
    zeros = jnp.zeros((q_ref.shape[0], HEAD_DIM), BF16)
    for i in range(n):
        qs_ref[i] = jnp.concatenate([q_ref[:, i * HEAD_DIM:(i + 1) * HEAD_DIM], zeros], axis=1)


def _merge_heads(acc_ref):
    outs = []
    for g in range(GROUP):
        a = acc_ref[g]
        outs.append(a / pltpu.roll(a, HEAD_DIM, axis=1))
    low = _iota((1, LANES), 1) < HEAD_DIM
    return jnp.where(low, outs[0], pltpu.roll(outs[1], HEAD_DIM, axis=1))


def _fox_kernel(qi_tab, kj_tab, q_ref, kv_ref, cq_ref, ck_ref, o_ref, qs_ref, m_ref, acc_ref, *, tq, tk):
    step = pl.program_id(2)
    qi, kj = qi_tab[step], kj_tab[step]

    @pl.when(kj == 0)
    def _():
        _init_flash(m_ref, acc_ref)
        _pad_heads(q_ref, qs_ref, GROUP)

    def update(d):
        k = kv_ref[:, :LANES]
        vext = kv_ref[:, LANES:]
        for g in range(GROUP):
            bias = (cq_ref[g][:, 0:1] - ck_ref[g]) * LOG2E
            logits = lambda r0, r1, c0, c1: _dot_t(qs_ref[g, r0:r1], k[c0:c1]) + bias[:, c0:c1]
            _tile_update(logits, vext, m_ref, acc_ref, g, tq, tk, None if d is None else d * tk)

    def finalize():
        o_ref[...] = _merge_heads(acc_ref).astype(o_ref.dtype)

    _when_tile(qi, kj, tq // tk, update, finalize)


def _fox_prompt(q_all, kvx, c_t, *, tq, tk):
    b, t, _ = q_all.shape
    qi_tab, kj_tab = _tri_tables(t // tq, tq // tk)
    xb = OFF_XB // KVX_AB
    grid_spec = pltpu.PrefetchScalarGridSpec(
        num_scalar_prefetch=2, grid=(b, KV_B, int(qi_tab.shape[0])),
        in_specs=[
            pl.BlockSpec((None, tq, LANES), lambda bi, h, s, qt, kt: (bi, qt[s], KV_A + h)),
            pl.BlockSpec((None, tk, KVX_AB), lambda bi, h, s, qt, kt: (bi, kt[s], xb + h)),
            pl.BlockSpec((None, GROUP, 1, tq), lambda bi, h, s, qt, kt: (bi, h, 0, qt[s])),
            pl.BlockSpec((None, GROUP, 1, tk), lambda bi, h, s, qt, kt: (bi, h, 0, kt[s])),
        ],
        out_specs=pl.BlockSpec((None, tq, LANES), lambda bi, h, s, qt, kt: (bi, qt[s], h)),
        scratch_shapes=[pltpu.VMEM((GROUP, tq, LANES), BF16), pltpu.VMEM((GROUP, tq, LANES), F32),
                        pltpu.VMEM((GROUP, tq, LANES), F32)])
    return pl.pallas_call(
        functools.partial(_fox_kernel, tq=tq, tk=tk), grid_spec=grid_spec,
        out_shape=jax.ShapeDtypeStruct((b, t, W_BRANCH), BF16), compiler_params=_params(3),
        name="fox_prompt")(qi_tab, kj_tab, q_all, kvx, c_t[:, :, None, :], c_t[:, :, None, :])


def _diff_kernel(qi_tab, kj_tab, slope_ref, q_ref, kv_ref, lam_ref, sg_ref, o_ref, qs_ref, m_ref, acc_ref, *, tq, tk,
                 lam_init):
    hk = pl.program_id(1)
    step = pl.program_id(2)
    qi, kj = qi_tab[step], kj_tab[step]

    @pl.when(kj == 0)
    def _():
        _init_flash(m_ref, acc_ref)
        _pad_heads(q_ref, qs_ref, 2 * GROUP)

    def update(d):
        vext = kv_ref[:, 2 * LANES:]
        rel = (kj * tk - qi * tq + _iota((1, tk), 1)).astype(F32)
        for g in range(GROUP):
            bias = slope_ref[hk * GROUP + g] * rel
            for m in range(2):
                i = m * GROUP + g
                logits = lambda r0, r1, c0, c1: (
                    _dot_t(qs_ref[i, r0:r1], kv_ref[c0:c1, m * LANES:(m + 1) * LANES]) + bias[:, c0:c1])
                _tile_update(logits, vext, m_ref, acc_ref, i, tq, tk, None if d is None else d * tk)

    def finalize():
        lam = lam_ref[...]
        lam_val = (jnp.exp(jnp.sum(lam[0:1] * lam[1:2], axis=1, keepdims=True))
                   - jnp.exp(jnp.sum(lam[2:3] * lam[3:4], axis=1, keepdims=True)) + lam_init)
        for g in range(GROUP):
            a0, a1 = acc_ref[g], acc_ref[GROUP + g]
            o = a0[:, :V_DIM_C] / a0[:, V_DIM_C:] - lam_val * (a1[:, :V_DIM_C] / a1[:, V_DIM_C:])
            ms = jnp.mean(o * o, axis=-1, keepdims=True)
            o = (o * lax.rsqrt(ms + RMS_EPS)) * sg_ref[...] * (1.0 - lam_init)
            o_ref[:, g * V_DIM_C:(g + 1) * V_DIM_C] = o.astype(o_ref.dtype)

    _when_tile(qi, kj, tq // tk, update, finalize)


def _diff_prompt(q_all, kvx, slopes, lam_l, sg_l, *, tq, tk, lam_init):
    b, t, _ = q_all.shape
    qi_tab, kj_tab = _tri_tables(t // tq, tq // tk)
    xc = OFF_XC // KVX_C
    grid_spec = pltpu.PrefetchScalarGridSpec(
        num_scalar_prefetch=2, grid=(b, KV_C, int(qi_tab.shape[0])),
        in_specs=[
            pl.BlockSpec(memory_space=pltpu.SMEM),
            pl.BlockSpec((None, tq, 2 * LANES), lambda bi, h, s, qt, kt: (bi, qt[s], 4 + h)),
            pl.BlockSpec((None, tk, KVX_C), lambda bi, h, s, qt, kt: (bi, kt[s], xc + h)),
            pl.BlockSpec((4, HEAD_DIM), lambda bi, h, s, qt, kt: (0, 0)),
            pl.BlockSpec((1, V_DIM_C), lambda bi, h, s, qt, kt: (0, 0)),
        ],
        out_specs=pl.BlockSpec((None, tq, 2 * LANES), lambda bi, h, s, qt, kt: (bi, qt[s], h)),
        scratch_shapes=[pltpu.VMEM((2 * GROUP, tq, LANES), BF16), pltpu.VMEM((2 * GROUP, tq, LANES), F32),
                        pltpu.VMEM((2 * GROUP, tq, 2 * V_DIM_C), F32)])
    return pl.pallas_call(
        functools.partial(_diff_kernel, tq=tq, tk=tk, lam_init=lam_init), grid_spec=grid_spec,
        out_shape=jax.ShapeDtypeStruct((b, t, W_BRANCH), BF16), compiler_params=_params(3),
        name="diff_prompt")(qi_tab, kj_tab, slopes, q_all, kvx, lam_l, sg_l)


def _topk_keep(gate, blk, own, past, nblk, axis):
    gm = jnp.where(past, gate, -jnp.inf)
    cnt = jnp.zeros(gate.shape, jnp.int32)
    for jp in range(nblk):
        gj = gm[jp:jp + 1, :] if axis == 0 else gm[:, jp:jp + 1]
        ahead = (gj > gm) | ((gj == gm) & (jp < blk))
        cnt = cnt + ahead.astype(jnp.int32)
    return (past & (cnt < MOBA_TOPK)) | (blk == own)


def _moba_kernel(qi_tab, kj_tab, slope_ref, q_ref, kv_ref, km_ref, o_ref, qa_ref, m_ref, acc_ref, *, tq, tk, nblk):
    hk = pl.program_id(1)
    step = pl.program_id(2)
    qi, kj = qi_tab[step], kj_tab[step]
    bpt = tk // MOBA_BLOCK
    nbp = -(-nblk // SUBLANES) * SUBLANES
    tile = tq

    @pl.when(kj == 0)
    def _():
        _init_flash(m_ref, acc_ref)
        own = lax.shift_right_logical(qi * tile + _iota((1, tile), 1), _log2(MOBA_BLOCK))
        blk = _iota((nbp, 1), 0)
        zeros = jnp.zeros((tile, HEAD_DIM), BF16)
        for g in range(GROUP):
            qg = q_ref[:, g * HEAD_DIM:(g + 1) * HEAD_DIM]
            gate = _dot_t(km_ref[:nbp, :HEAD_DIM], qg.astype(F32), precision=lax.Precision.HIGHEST)
            keep = _topk_keep(gate, blk, own, blk < own, nblk, 0)
            bias = jnp.where(keep, 0.0, NEG)
            bias = jnp.concatenate([bias, jnp.zeros((LANES - nbp, tile), F32)], axis=0)
            qa_ref[g, :, :LANES] = jnp.concatenate([qg, zeros], axis=1)
            qa_ref[g, :, LANES:] = bias.T.astype(BF16)

    def update(d):
        key_blk = kj * bpt + lax.shift_right_logical(_iota((tk, 1), 0), _log2(MOBA_BLOCK))
        onehot = jnp.where(_iota((1, LANES), 1) == key_blk, 1.0, 0.0).astype(BF16)
        kaug = jnp.concatenate([kv_ref[:, :LANES], onehot], axis=1)
        vext = kv_ref[:, LANES:]
        rel = (kj * tk - qi * tq + _iota((1, tk), 1)).astype(F32)
        for g in range(GROUP):
            bias = slope_ref[hk * GROUP + g] * rel
            logits = lambda r0, r1, c0, c1: _dot_t(qa_ref[g, r0:r1], kaug[c0:c1]) + bias[:, c0:c1]
            _tile_update(logits, vext, m_ref, acc_ref, g, tq, tk, None if d is None else d * tk)

    def finalize():
        o_ref[...] = _merge_heads(acc_ref).astype(o_ref.dtype)

    _when_tile(qi, kj, tq // tk, update, finalize)


def _moba_prompt(q_all, kvx, kmean_pad, slopes, *, tq, tk):
    b, t, _ = q_all.shape
    qi_tab, kj_tab = _tri_tables(t // tq, tq // tk)
    grid_spec = pltpu.PrefetchScalarGridSpec(
        num_scalar_prefetch=2, grid=(b, KV_A, int(qi_tab.shape[0])),
        in_specs=[
            pl.BlockSpec(memory_space=pltpu.SMEM),
            pl.BlockSpec((None, tq, LANES), lambda bi, h, s, qt, kt: (bi, qt[s], h)),
            pl.BlockSpec((None, tk, KVX_AB), lambda bi, h, s, qt, kt: (bi, kt[s], h)),
            pl.BlockSpec((None, LANES, KVX_AB), lambda bi, h, s, qt, kt: (bi, 0, h)),
        ],
        out_specs=pl.BlockSpec((None, tq, LANES), lambda bi, h, s, qt, kt: (bi, qt[s], h)),
        scratch_shapes=[pltpu.VMEM((GROUP, tq, 2 * LANES), BF16), pltpu.VMEM((GROUP, tq, LANES), F32),
                        pltpu.VMEM((GROUP, tq, LANES), F32)])
    return pl.pallas_call(
        functools.partial(_moba_kernel, tq=tq, tk=tk, nblk=t // MOBA_BLOCK), grid_spec=grid_spec,
        out_shape=jax.ShapeDtypeStruct((b, t, W_BRANCH), BF16), compiler_params=_params(3),
        name="moba_prompt")(qi_tab, kj_tab, slopes, q_all, kvx, kmean_pad)


def _rows_of(q_ref, cols):
    return jnp.concatenate([q_ref[:, c:c + HEAD_DIM] for c in cols], axis=0)


def _pad_q(q, front=False):
    zeros = jnp.zeros(q.shape, BF16)
    return jnp.concatenate([zeros, q.astype(BF16)] if front else [q.astype(BF16), zeros], axis=1)


def _col_per_group(vals, n_tok):
    return jnp.concatenate([jnp.full((n_tok, 1), v, F32) for v in vals], axis=0)


def _rows_per_group(x, n_tok):
    return jnp.concatenate([jnp.broadcast_to(x[i:i + 1], (n_tok, x.shape[1])) for i in range(x.shape[0])], axis=0)


def _head_match(rows, cols, nkv, n_tok, col_head):
    row_head = lax.shift_right_logical(_iota((rows, 1), 0), _log2(GROUP * n_tok)) & (nkv - 1)
    return jnp.where(col_head == row_head, 0.0, NEG)


def _new_rows(kvn_ref, nkv):
    w = kvn_ref.shape[1] // nkv
    return jnp.concatenate([kvn_ref[:, h * w:(h + 1) * w] for h in range(nkv)], axis=0)


def _new_bias(rows, nkv, n_tok):
    col = _iota((1, nkv * n_tok), 1)
    bias = _head_match(rows, nkv * n_tok, nkv, n_tok, lax.shift_right_logical(col, _log2(n_tok)))
    visible = (col & (n_tok - 1)) <= (_iota((rows, 1), 0) & (n_tok - 1))
    return jnp.where(visible, bias, NEG), (col & (n_tok - 1)).astype(F32)


def _multi_update(s_list, v_list, m_ref, l_ref, acc_ref, shift=0):
    idx = ...
    m_prev = m_ref[idx]
    smax = s_list[0]
    for s in s_list[1:]:
        smax = jnp.maximum(smax, s)
    m_next = jnp.maximum(m_prev, jnp.max(smax, axis=1, keepdims=True))
    alpha = jnp.exp2(m_prev - m_next)
    psum, pv = None, None
    for s, v in zip(s_list, v_list):
        p = jnp.exp2(s - m_next)
        pm = pltpu.roll(p, shift, axis=1) if shift else p
        d = jnp.dot(pm.astype(BF16), v, preferred_element_type=F32)
        psum = p if psum is None else psum + p
        pv = d if pv is None else pv + d
    l_ref[idx] = alpha * l_ref[idx] + jnp.sum(psum, axis=1, keepdims=True)
    acc_ref[idx] = alpha * acc_ref[idx] + pv
    m_ref[idx] = m_next


def _init_multi(m_ref, l_ref, acc_ref):
    m_ref[...] = jnp.full(m_ref.shape, NEG, F32)
    l_ref[...] = jnp.zeros(l_ref.shape, F32)
    acc_ref[...] = jnp.zeros(acc_ref.shape, F32)


def _fox_sample_kernel(pt_ref, q_ref, kvn_ref, lfn_ref, *rest, npg, n_tok):
    pages, lfs = rest[:npg], rest[npg:2 * npg]
    o_ref, m_ref, l_ref, acc_ref, carry_ref = rest[2 * npg:]
    j = pl.program_id(1)
    rows, pr = H_B * n_tok, PAGE_SIZE * KV_B

    @pl.when(j == 0)
    def _():
        _init_multi(m_ref, l_ref, acc_ref)
        carry_ref[...] = jnp.zeros(carry_ref.shape, F32)

    q = _pad_q(_rows_of(q_ref, [h * HEAD_DIM for h in range(H_B)]))
    col = _iota((1, pr), 1)
    match = _head_match(rows, pr, KV_B, n_tok, col & (KV_B - 1))
    upper = (_iota((PAGE_SIZE, 1), 0) <= lax.shift_right_logical(col, _log2(KV_B))).astype(F32)
    carry = carry_ref[...]
    s_list, v_list = [], []
    c_local = jnp.dot(jnp.concatenate([lf[...] for lf in lfs], axis=0), upper, preferred_element_type=F32,
                      precision=lax.Precision.HIGHEST)
    for r in range(npg):
        c_full = c_local[r * H_B:(r + 1) * H_B] + carry
        carry = c_full[:, pr - 1:pr]
        bias = match - _rows_per_group(c_full, n_tok) * LOG2E
        v_list.append(pages[r][...].astype(BF16))
        s_list.append(_dot_t(q, v_list[-1]) + bias)
    carry_ref[...] = carry
    _multi_update(s_list, v_list, m_ref, l_ref, acc_ref)

    @pl.when(j == pl.num_programs(1) - 1)
    def _():
        kvn = _new_rows(kvn_ref, KV_B).astype(BF16)
        bias, _ = _new_bias(rows, KV_B, n_tok)
        tok = _iota((1, KV_B * n_tok), 1) & (n_tok - 1)
        upper_n = (_iota((n_tok, 1), 0) <= tok).astype(F32)
        c_new = _dot_tl(lfn_ref[...], upper_n) + carry
        s = _dot_t(q, kvn) + (bias - _rows_per_group(c_new, n_tok) * LOG2E)
        _multi_update([s], [kvn], m_ref, l_ref, acc_ref)
        o = acc_ref[:, HEAD_DIM:] / l_ref[...]
        for h in range(H_B):
            o_ref[:, h * HEAD_DIM:(h + 1) * HEAD_DIM] = o[h * n_tok:(h + 1) * n_tok]


def _diff_sample_kernel(pt_ref, slope_ref, lam_ref, sg_ref, q_ref, kvn_ref, *rest, npg, n_tok, lam_init):
    pages = rest[:npg]
    o_ref, m_ref, l_ref, acc_ref = rest[npg:]
    j = pl.program_id(1)
    half = H_C * n_tok
    rows, pr = 2 * half, 2 * PAGE_SIZE * KV_C
    n_past = pl.num_programs(1) * npg * PAGE_SIZE

    @pl.when(j == 0)
    def _():
        _init_multi(m_ref, l_ref, acc_ref)

    qs = [_rows_of(q_ref, [((hk * 2 + m) * GROUP + g) * HEAD_DIM for hk in range(KV_C) for g in range(GROUP)])
          for m in range(2)]
    qpad = [_pad_q(qs[0]), _pad_q(qs[1], front=True)]
    slope = _col_per_group([slope_ref[h] for h in range(H_C)] * 2, n_tok)

    def logits(kv, bias):
        return jnp.concatenate([_dot_t(qs[m].astype(BF16), kv[:, m * HEAD_DIM:(m + 1) * HEAD_DIM].astype(BF16))
                                for m in range(2)], axis=0) + bias

    col = _iota((1, pr), 1)
    key_head = jnp.where((col & KV_C) == 0, col & (KV_C - 1), -1)
    match = _head_match(rows, pr, KV_C, n_tok, key_head)
    tok = lax.shift_right_logical(col, _log2(2 * KV_C))
    s_list, v_list = [], []
    for r in range(npg):
        rel = ((j * npg + r) * PAGE_SIZE - n_past + tok).astype(F32)
        v_list.append(pages[r][...].astype(BF16))
        s_list.append(jnp.concatenate([_dot_t(qp, v_list[-1]) for qp in qpad], axis=0) + (match + slope * rel))
    _multi_update(s_list, v_list, m_ref, l_ref, acc_ref, shift=KV_C)

    @pl.when(j == pl.num_programs(1) - 1)
    def _():
        kvn = _new_rows(kvn_ref, KV_C)
        bias, rel = _new_bias(rows, KV_C, n_tok)
        _multi_update([logits(kvn, bias + slope * rel)], [kvn[:, 2 * HEAD_DIM:].astype(BF16)], m_ref, l_ref, acc_ref)
        lam = lam_ref[...]
        lam_val = (jnp.exp(jnp.sum(lam[0:1] * lam[1:2], axis=1, keepdims=True))
                   - jnp.exp(jnp.sum(lam[2:3] * lam[3:4], axis=1, keepdims=True)) + lam_init)
        on = acc_ref[...] / l_ref[...]
        o = on[:half] - lam_val * on[half:]
        ms = jnp.mean(o * o, axis=-1, keepdims=True)
        o = (o * lax.rsqrt(ms + RMS_EPS)) * sg_ref[...] * (1.0 - lam_init)
        for h in range(H_C):
            o_ref[:, h * V_DIM_C:(h + 1) * V_DIM_C] = o[h * n_tok:(h + 1) * n_tok]


def _moba_sample_kernel(pt_ref, slope_ref, q_ref, kvn_ref, *rest, npg, n_tok, nblk):
    pages = rest[:npg]
    o_ref, mb_ref, lb_ref, gb_ref, ob_ref = rest[npg:]
    j = pl.program_id(1)
    rows, pr = H_A * n_tok, PAGE_SIZE * KV_A
    ppb = MOBA_BLOCK // PAGE_SIZE
    bps = npg // ppb
    lane = _iota((1, LANES), 1)
    n_past = pl.num_programs(1) * npg * PAGE_SIZE

    @pl.when(j == 0)
    def _():
        mb_ref[...] = jnp.full(mb_ref.shape, NEG, F32)
        lb_ref[...] = jnp.zeros(lb_ref.shape, F32)
        gb_ref[...] = jnp.zeros(gb_ref.shape, F32)

    qf = _rows_of(q_ref, [h * HEAD_DIM for h in range(H_A)])
    q = _pad_q(qf)
    slope = _col_per_group([slope_ref[h] for h in range(H_A)], n_tok)
    col = _iota((1, pr), 1)
    match = _head_match(rows, pr, KV_A, n_tok, col & (KV_A - 1))
    tok = lax.shift_right_logical(col, _log2(KV_A))
    row_kv = lax.shift_right_logical(_iota((rows, 1), 0), _log2(GROUP * n_tok))

    for bl in range(bps):
        blk = j * bps + bl
        s_l, v_l, kmean = [], [], None
        for r in range(ppb):
            page = pages[bl * ppb + r]
            rel = ((blk * ppb + r) * PAGE_SIZE - n_past + tok).astype(F32)
            v_l.append(page[...].astype(BF16))
            s_l.append(_dot_t(q, v_l[-1]) + (match + slope * rel))
            km = jnp.sum(page[:, :HEAD_DIM].reshape(pr // SUBLANES, SUBLANES, HEAD_DIM), axis=0)
            kmean = km if kmean is None else kmean + km
        mb = jnp.max(functools.reduce(jnp.maximum, s_l), axis=1, keepdims=True)
        ps = [jnp.exp2(s - mb) for s in s_l]
        lb = jnp.sum(functools.reduce(jnp.add, ps), axis=1, keepdims=True)
        ob_ref[blk] = functools.reduce(
            jnp.add, [jnp.dot(p.astype(BF16), v, preferred_element_type=F32) for p, v in zip(ps, v_l)])
        kmean = (kmean[:KV_A] + kmean[KV_A:]) * (1.0 / MOBA_BLOCK)
        kmean = jnp.concatenate([kmean, jnp.zeros((SUBLANES - KV_A, HEAD_DIM), F32)], axis=0)
        gate_all = _dot_t(qf, kmean, precision=lax.Precision.HIGHEST)
        gate = jnp.sum(jnp.where(_iota((1, SUBLANES), 1) == row_kv, gate_all, 0.0), axis=1, keepdims=True)
        here = lane == blk
        mb_ref[...] = jnp.where(here, mb, mb_ref[...])
        lb_ref[...] = jnp.where(here, lb, lb_ref[...])
        gb_ref[...] = jnp.where(here, gate, gb_ref[...])

    @pl.when(j == pl.num_programs(1) - 1)
    def _():
        keep = _topk_keep(gb_ref[...], lane, nblk, lane < nblk, nblk, 1) & (lane < nblk)
        m_sel = jnp.where(keep, mb_ref[...], NEG)
        kvn = _new_rows(kvn_ref, KV_A).astype(BF16)
        bias, rel = _new_bias(rows, KV_A, n_tok)
        s = _dot_t(q, kvn) + (bias + slope * rel)
        m_tot = jnp.maximum(jnp.max(s, axis=1, keepdims=True), jnp.max(m_sel, axis=1, keepdims=True))
        p = jnp.exp2(s - m_tot)
        w = jnp.where(keep, jnp.exp2(m_sel - m_tot), 0.0)
        den = jnp.sum(p, axis=1, keepdims=True) + jnp.sum(w * lb_ref[...], axis=1, keepdims=True)
        num = jnp.dot(p.astype(BF16), kvn, preferred_element_type=F32)
        for b in range(nblk):
            num = num + w[:, b:b + 1] * ob_ref[b]
        o = num[:, HEAD_DIM:] / den
        for h in range(H_A):
            o_ref[:, h * HEAD_DIM:(h + 1) * HEAD_DIM] = o[h * n_tok:(h + 1) * n_tok]


def _page_specs(layer, n_pages, npg, page_shape):
    zeros = (0,) * len(page_shape)

    def spec(r):
        return pl.BlockSpec((None, None) + tuple(page_shape),
                            lambda b, j, pt: (layer, pt[b * n_pages + j * npg + r]) + zeros)
    return [spec(r) for r in range(npg)]


def _sample_attention(kind, layer, pt_flat, q_s, kvn, cache, *, n_pages, extra=(), extra_specs=(), logf_cache=None,
                      lfn=None, lam_init=0.0):
    db, n_tok, _ = q_s.shape
    npg = PAGES_PER_STEP if n_pages % PAGES_PER_STEP == 0 else MOBA_BLOCK // PAGE_SIZE
    per_b = lambda w: pl.BlockSpec((None, n_tok, w), lambda b, j, pt: (b, 0, 0))
    qcol = {"moba": 0, "fox": 1, "diff": 2}[kind]
    q_spec = pl.BlockSpec((None, n_tok, W_BRANCH), lambda b, j, pt: (b, 0, qcol))
    in_specs = list(extra_specs) + [q_spec, per_b(W_BRANCH)]
    args = list(extra) + [q_s, kvn]
    if kind == "fox":
        in_specs.append(per_b(H_B))
        args.append(lfn)
    in_specs += _page_specs(layer, n_pages, npg, cache.shape[2:])
    args += [cache] * npg
    col = lambda r, w: pltpu.VMEM((r, w), F32)
    if kind == "fox":
        rows = H_B * n_tok
        in_specs += _page_specs(layer, n_pages, npg, (H_B, PAGE_SIZE))
        args += [logf_cache] * npg
        kern = functools.partial(_fox_sample_kernel, npg=npg, n_tok=n_tok)
        scratch = [col(rows, 1), col(rows, 1), col(rows, 2 * HEAD_DIM), col(H_B, 1)]
    elif kind == "diff":
        rows = 2 * H_C * n_tok
        kern = functools.partial(_diff_sample_kernel, npg=npg, n_tok=n_tok, lam_init=lam_init)
        scratch = [col(rows, 1), col(rows, 1), col(rows, V_DIM_C)]
    else:
        rows = H_A * n_tok
        nblk = n_pages * PAGE_SIZE // MOBA_BLOCK
        kern = functools.partial(_moba_sample_kernel, npg=npg, n_tok=n_tok, nblk=nblk)
        scratch = [col(rows, LANES), col(rows, LANES), col(rows, LANES), pltpu.VMEM((nblk, rows, 2 * HEAD_DIM), F32)]
    grid_spec = pltpu.PrefetchScalarGridSpec(
        num_scalar_prefetch=1, grid=(db, n_pages // npg), in_specs=in_specs, out_specs=per_b(W_BRANCH),
        scratch_shapes=scratch)
    return pl.pallas_call(kern, grid_spec=grid_spec, out_shape=jax.ShapeDtypeStruct((db, n_tok, W_BRANCH), F32),
                          compiler_params=_params(2), name=kind + "_sample")(pt_flat, *args)


def _alibi_slopes(n):
    return 2.0 ** (-8.0 * jnp.arange(1, n + 1, dtype=F32) / n)


def kernel(x_prompt, x_sample, cache_a_kv, cache_b_kv, cache_b_logf, cache_c_kv, page_table, w_in, w_o_a, w_o_b,
           w_o_c, w_out, norm_g, forget_b, diff_lambda, diff_subln_g, final_norm_g):
    depth = w_in.shape[0]
    bp, t, _ = x_prompt.shape
    db, n_tok, _ = x_sample.shape
    n_pages = page_table.shape[1]
    n_pool = cache_a_kv.shape[1]
    assert t % MOBA_BLOCK == 0 and (n_pages * PAGE_SIZE) % MOBA_BLOCK == 0
    assert t // MOBA_BLOCK <= LANES and n_pages * PAGE_SIZE // MOBA_BLOCK <= LANES
    assert n_tok == SUBLANES

    w_perm = _permute_columns(w_in)
    woa, wob, woc, wout = (w.astype(BF16) for w in (w_o_a, w_o_b, w_o_c, w_out))
    norm_g3 = norm_g.reshape(depth, 1, D_MODEL)
    fb3 = jnp.pad(forget_b, ((0, 0), (0, FB_PAD - H_B))).reshape(depth, 1, FB_PAD)
    fg2 = final_norm_g.reshape(1, D_MODEL)
    slopes_a, slopes_c = _alibi_slopes(H_A) * LOG2E, _alibi_slopes(H_C) * LOG2E
    pt_flat = page_table.reshape(-1).astype(jnp.int32)
    ca = cache_a_kv.reshape(depth, n_pool, PAGE_SIZE * KV_A, 2 * HEAD_DIM)
    cb = cache_b_kv.reshape(depth, n_pool, PAGE_SIZE * KV_B, 2 * HEAD_DIM)
    logf_t = jnp.swapaxes(cache_b_logf, 2, 3)
    cc = cache_c_kv.reshape(depth, n_pool, PAGE_SIZE, KV_C, 2, LANES).transpose(0, 1, 2, 4, 3, 5)
    cc = cc.reshape(depth, n_pool, 2 * PAGE_SIZE * KV_C, LANES)

    tm_p = 256 if (bp * t) % 256 == 0 else bp * t
    tk = next((c for c in (1024, 512) if t % c == 0), t)
    tq = Q_TILE_MULT * tk if t % (Q_TILE_MULT * tk) == 0 else tk
    ns = db * n_tok
    smem = pl.BlockSpec(memory_space=pltpu.SMEM)

    xp = x_prompt.reshape(bp * t, D_MODEL)
    xs = x_sample.reshape(ns, D_MODEL)
    outs_p = [[], [], [], []]
    outs_s = [[], [], [], []]
    for l in range(depth):
        lam_init = 0.8 - 0.6 * math.exp(-0.3 * l)
        sg = diff_subln_g[l].reshape(1, V_DIM_C)
        final = l == depth - 1

        kva, kvb, kvc, logf, kvx, q_all, gates, gm = _inproj(xp, l, norm_g3, fb3, w_perm, tm=tm_p, q_dtype=BF16,
                                                             split_heads=True)
        for acc, v in zip(outs_p, (kva, kvb, logf, kvc)):
            acc.append(v)
        q3 = q_all.reshape(bp, t, 3 * W_BRANCH)
        kv3 = kvx.reshape(bp, t, W_KVX)
        km = _kmean(kvx).reshape(bp, t // MOBA_BLOCK, KV_A * KVX_AB)
        km = jnp.pad(km, ((0, 0), (0, LANES - t // MOBA_BLOCK), (0, 0)))
        c_t = _cumsum_t(jnp.swapaxes(logf.reshape(bp, t, H_B), 1, 2))
        oa = _moba_prompt(q3, kv3, km, slopes_a, tq=tq, tk=tk)
        ob = _fox_prompt(q3, kv3, c_t, tq=tq, tk=tk)
        oc = _diff_prompt(q3, kv3, slopes_c, diff_lambda[l], sg, tq=tk, tk=tk, lam_init=lam_init)
        xp = _outproj(xp, oa.reshape(bp * t, W_BRANCH), ob.reshape(bp * t, W_BRANCH), oc.reshape(bp * t, W_BRANCH),
                      gates, gm, l, woa, wob, woc, wout, fg2, tm=tm_p, final=final)

        kva, kvb, kvc, logf, _, q_all, gates, gm = _inproj(xs, l, norm_g3, fb3, w_perm, tm=ns, q_dtype=F32,
                                                           split_heads=False)
        for acc, v in zip(outs_s, (kva, kvb, logf, kvc)):
            acc.append(v)
        q3 = q_all.reshape(db, n_tok, 3 * W_BRANCH)
        r3 = lambda a: a.reshape(db, n_tok, a.shape[-1])
        oa = _sample_attention("moba", l, pt_flat, q3, r3(kva), ca, n_pages=n_pages, extra=(slopes_a,),
                               extra_specs=(smem,))
        ob = _sample_attention("fox", l, pt_flat, q3, r3(kvb), cb, n_pages=n_pages, logf_cache=logf_t,
                               lfn=r3(logf))
        oc = _sample_attention("diff", l, pt_flat, q3, r3(kvc), cc, n_pages=n_pages,
                               extra=(slopes_c, diff_lambda[l], sg),
                               extra_specs=(smem, pl.BlockSpec((4, HEAD_DIM), lambda b, j, pt: (0, 0)),
                                            pl.BlockSpec((1, V_DIM_C), lambda b, j, pt: (0, 0))),
                               lam_init=lam_init)
        xs = _outproj(xs, oa.reshape(ns, W_BRANCH), ob.reshape(ns, W_BRANCH), oc.reshape(ns, W_BRANCH), gates, gm, l,
                      woa, wob, woc, wout, fg2, tm=ns, final=final)

    def stack(vals, b_, t_, tail):
        return jnp.stack(vals).reshape((depth, b_, t_) + tail)

    pa, pb, pf, pc = outs_p
    sa, sb, sf, sc = outs_s
    kv_tail, c_tail = (KV_A, 2 * HEAD_DIM), (KV_C, 2 * HEAD_DIM + V_DIM_C)
    return (xp.reshape(bp, t, D_MODEL), xs.reshape(db, n_tok, D_MODEL),
            stack(pa, bp, t, kv_tail), stack(pb, bp, t, kv_tail), stack(pf, bp, t, (H_B,)), stack(pc, bp, t, c_tail),
            stack(sa, db, n_tok, kv_tail), stack(sb, db, n_tok, kv_tail), stack(sf, db, n_tok, (H_B,)),
            stack(sc, db, n_tok, c_tail))
```

```python
import functools
import math

import jax
import jax.numpy as jnp
import numpy as np
from jax import lax
from jax.experimental import pallas as pl
from jax.experimental.pallas import tpu as pltpu

F32 = jnp.float32
BF16 = jnp.bfloat16

D_MODEL = 1024
HEAD_DIM = 64
PAGE_SIZE = 128
H_A, KV_A = 8, 4
H_B, KV_B = 8, 4
H_C, KV_C = 4, 2
GROUP = 2
V_DIM_C = 2 * HEAD_DIM
MOBA_BLOCK = 256
MOBA_TOPK = 3
N_BRANCH = 3
RMS_EPS = 1e-6
W_BRANCH = 512
IN_NAMES = ("qa", "ka", "va", "ga", "qb", "kb", "vb", "fb", "gb", "qc", "kc", "vc", "gc", "gm")
IN_SIZES = (512, 256, 256, 512, 512, 256, 256, 8, 512, 512, 256, 256, 512, N_BRANCH * D_MODEL)
D_IN = sum(IN_SIZES)

NEG = -1e30
LOG2E = math.log2(math.e)
LANES = 128
SUBLANES = 8
FB_PAD = LANES
OFF_KVA, OFF_KVB, OFF_KVC, OFF_FB = 0, 512, 1024, 1536
OFF_Q = OFF_FB + FB_PAD
OFF_G = OFF_Q + 3 * W_BRANCH
OFF_GM = OFF_G + 3 * W_BRANCH
D_PERM = OFF_GM + N_BRANCH * D_MODEL
KVX_AB = 4 * HEAD_DIM
KVX_C = 8 * HEAD_DIM
OFF_XA, OFF_XB, OFF_XC = 0, KV_A * KVX_AB, KV_A * KVX_AB + KV_B * KVX_AB
W_KVX = OFF_XC + KV_C * KVX_C
VMEM_LIMIT = 56 * 1024 * 1024
Q_TILE_MULT = 2
PAGES_PER_STEP = 32


def _perm_and_scale():
    off = dict(zip(IN_NAMES, np.concatenate([[0], np.cumsum(IN_SIZES)[:-1]])))
    d = HEAD_DIM
    cols = []
    for kn, vn, nkv in (("ka", "va", KV_A), ("kb", "vb", KV_B)):
        for h in range(nkv):
            cols += list(range(off[kn] + h * d, off[kn] + (h + 1) * d))
            cols += list(range(off[vn] + h * d, off[vn] + (h + 1) * d))
    for h in range(KV_C):
        for m in range(2):
            cols += list(range(off["kc"] + (m * KV_C + h) * d, off["kc"] + (m * KV_C + h + 1) * d))
        cols += list(range(off["vc"] + h * V_DIM_C, off["vc"] + (h + 1) * V_DIM_C))
    cols += list(range(off["fb"], off["fb"] + H_B)) + [D_IN] * (FB_PAD - H_B)
    cols += list(range(off["qa"], off["qa"] + 512))
    cols += list(range(off["qb"], off["qb"] + 512))
    for h in range(KV_C):
        for m in range(2):
            for g in range(GROUP):
                s = off["qc"] + ((m * KV_C + h) * GROUP + g) * d
                cols += list(range(s, s + d))
    for n in ("ga", "gb", "gc"):
        cols += list(range(off[n], off[n] + 512))
    cols += list(range(off["gm"], off["gm"] + N_BRANCH * D_MODEL))
    perm = np.asarray(cols, np.int32)
    assert perm.shape[0] == D_PERM
    scale = np.ones((D_PERM,), np.float32)
    scale[OFF_Q:OFF_G] = HEAD_DIM ** -0.5 * LOG2E
    return perm, scale


_PERM, _SCALE = _perm_and_scale()


def _permute_columns(w_in):
    pieces, start = [], 0
    for i in range(1, D_PERM + 1):
        joined = i < D_PERM and _SCALE[i] == _SCALE[start] and (
            _PERM[i] == _PERM[i - 1] + 1 or _PERM[i] == _PERM[i - 1] == D_IN)
        if not joined:
            a, n = int(_PERM[start]), i - start
            if a == D_IN:
                pieces.append(jnp.zeros(w_in.shape[:2] + (n,), BF16))
            else:
                pieces.append((w_in[:, :, a:a + n] * float(_SCALE[start])).astype(BF16))
            start = i
    return jnp.concatenate(pieces, axis=2)


def _params(n_axes):
    return pltpu.CompilerParams(dimension_semantics=("arbitrary",) * n_axes, vmem_limit_bytes=VMEM_LIMIT)


def _sigmoid(x):
    return 1.0 / (1.0 + jnp.exp(-x))


def _dot_t(a, b, precision=None):
    return lax.dot_general(a, b, (((1,), (1,)), ((), ())), preferred_element_type=F32, precision=precision)


def _dot_tl(a, b):
    return lax.dot_general(a, b, (((0,), (0,)), ((), ())), preferred_element_type=F32,
                           precision=lax.Precision.HIGHEST)


def _iota(shape, axis):
    return lax.broadcasted_iota(jnp.int32, shape, axis)


def _log2(n):
    assert n & (n - 1) == 0
    return n.bit_length() - 1


def _store_heads(ref, z):
    if len(ref.shape) == 2:
        ref[...] = z
        return
    w = ref.shape[2]
    for h in range(ref.shape[1]):
        ref[:, h, :] = z[:, h * w:(h + 1) * w]


def _inproj_kernel(x_ref, g_ref, fb_ref, w_ref, kva_ref, kvb_ref, kvc_ref, logf_ref, kvx_ref, q_ref, gate_ref,
                   gm_ref):
    x = x_ref[...]
    ms = jnp.mean(x * x, axis=-1, keepdims=True)
    h = ((x * lax.rsqrt(ms + RMS_EPS)) * g_ref[...]).astype(BF16)

    def mm(a, n):
        return jnp.dot(h, w_ref[:, a:a + n], preferred_element_type=F32)

    low = (_iota((1, W_BRANCH), 1) & (LANES - 1)) < HEAD_DIM
    for i, (ref, off_x) in enumerate(((kva_ref, OFF_XA), (kvb_ref, OFF_XB))):
        z = mm(i * W_BRANCH, W_BRANCH)
        _store_heads(ref, z)
        kx = jnp.where(low, z, 0.0).astype(BF16)
        vx = jnp.where(low, pltpu.roll(z, W_BRANCH - HEAD_DIM, axis=1), 1.0).astype(BF16)
        for hk in range(KV_A):
            kvx_ref[:, off_x + hk * KVX_AB: off_x + hk * KVX_AB + LANES] = kx[:, hk * LANES:(hk + 1) * LANES]
            kvx_ref[:, off_x + hk * KVX_AB + LANES: off_x + (hk + 1) * KVX_AB] = vx[:, hk * LANES:(hk + 1) * LANES]
    z = mm(OFF_KVC, W_BRANCH)
    _store_heads(kvc_ref, z)
    k1 = jnp.where(low, z, 0.0).astype(BF16)
    k2 = jnp.where(low, pltpu.roll(z, W_BRANCH - HEAD_DIM, axis=1), 0.0).astype(BF16)
    for hk in range(KV_C):
        c, o = hk * 2 * LANES, OFF_XC + hk * KVX_C
        kvx_ref[:, o: o + LANES] = k1[:, c:c + LANES]
        kvx_ref[:, o + LANES: o + 2 * LANES] = k2[:, c:c + LANES]
        kvx_ref[:, o + 2 * LANES: o + 3 * LANES] = z[:, c + LANES: c + 2 * LANES].astype(BF16)
        kvx_ref[:, o + 3 * LANES: o + 4 * LANES] = jnp.ones((z.shape[0], LANES), BF16)
    zf = mm(OFF_FB, FB_PAD) + fb_ref[...]
    lf = jnp.minimum(zf, 0.0) - jnp.log(1.0 + jnp.exp(-jnp.abs(zf)))
    logf_ref[...] = lf[:, :H_B]
    for i in range(3):
        q_ref[:, i * W_BRANCH:(i + 1) * W_BRANCH] = mm(OFF_Q + i * W_BRANCH, W_BRANCH).astype(q_ref.dtype)
        gate_ref[:, i * W_BRANCH:(i + 1) * W_BRANCH] = mm(OFF_G + i * W_BRANCH, W_BRANCH).astype(BF16)
    for i in range(N_BRANCH * D_MODEL // W_BRANCH):
        gm_ref[:, i * W_BRANCH:(i + 1) * W_BRANCH] = mm(OFF_GM + i * W_BRANCH, W_BRANCH).astype(BF16)


def _inproj(x2d, layer, norm_g3, fb3, w_perm, *, tm, q_dtype, split_heads):
    n = x2d.shape[0]
    row = lambda w: pl.BlockSpec((tm, w), lambda i: (i, 0))
    lay = lambda shp: pl.BlockSpec((None,) + shp, lambda i: (layer, 0, 0))
    kv_shapes = [(KV_A, 2 * HEAD_DIM), (KV_B, 2 * HEAD_DIM), (KV_C, 2 * HEAD_DIM + V_DIM_C)]
    if split_heads:
        kv_out = [jax.ShapeDtypeStruct((n,) + shp, F32) for shp in kv_shapes]
        kv_specs = [pl.BlockSpec((tm,) + shp, lambda i: (i, 0, 0)) for shp in kv_shapes]
    else:
        kv_out = [jax.ShapeDtypeStruct((n, W_BRANCH), F32)] * 3
        kv_specs = [row(W_BRANCH)] * 3
    out_shape = (
        *kv_out, jax.ShapeDtypeStruct((n, H_B), F32),
        jax.ShapeDtypeStruct((n, W_KVX), BF16), jax.ShapeDtypeStruct((n, 3 * W_BRANCH), q_dtype),
        jax.ShapeDtypeStruct((n, 3 * W_BRANCH), BF16), jax.ShapeDtypeStruct((n, N_BRANCH * D_MODEL), BF16))
    return pl.pallas_call(
        _inproj_kernel, grid=(n // tm,),
        in_specs=[row(D_MODEL), lay((1, D_MODEL)), lay((1, FB_PAD)), lay((D_MODEL, D_PERM))],
        out_specs=[*kv_specs, row(H_B), row(W_KVX), row(3 * W_BRANCH), row(3 * W_BRANCH), row(N_BRANCH * D_MODEL)],
        out_shape=out_shape, compiler_params=_params(1), name="inproj")(x2d, norm_g3, fb3, w_perm)


def _outproj_kernel(x_ref, oa_ref, ob_ref, oc_ref, gate_ref, gm_ref, woa_ref, wob_ref, woc_ref, wout_ref, fg_ref,
                    y_ref, *, final):
    def branch(i, o_ref, w_ref):
        g = gate_ref[:, i * W_BRANCH:(i + 1) * W_BRANCH].astype(F32)
        u = (o_ref[...].astype(F32) * (g * _sigmoid(g))).astype(BF16)
        p = jnp.dot(u, w_ref[...], preferred_element_type=F32)
        return _sigmoid(gm_ref[:, i * D_MODEL:(i + 1) * D_MODEL].astype(F32)) * p

    merged = branch(0, oa_ref, woa_ref) + branch(1, ob_ref, wob_ref) + branch(2, oc_ref, woc_ref)
    y = x_ref[...] + jnp.dot(merged.astype(BF16), wout_ref[...], preferred_element_type=F32)
    if final:
        ms = jnp.mean(y * y, axis=-1, keepdims=True)
        y = (y * lax.rsqrt(ms + RMS_EPS)) * fg_ref[...]
    y_ref[...] = y


def _outproj(x2d, oa, ob, oc, gates, gm, layer, woa, wob, woc, wout, fg2, *, tm, final):
    n = x2d.shape[0]
    row = lambda w: pl.BlockSpec((tm, w), lambda i: (i, 0))
    lay = lambda shp: pl.BlockSpec((None,) + shp, lambda i: (layer, 0, 0))
    return pl.pallas_call(
        functools.partial(_outproj_kernel, final=final), grid=(n // tm,),
        in_specs=[row(D_MODEL), row(W_BRANCH), row(W_BRANCH), row(W_BRANCH), row(3 * W_BRANCH),
                  row(N_BRANCH * D_MODEL), lay((W_BRANCH, D_MODEL)), lay((W_BRANCH, D_MODEL)),
                  lay((W_BRANCH, D_MODEL)), lay((D_MODEL, D_MODEL)), pl.BlockSpec((1, D_MODEL), lambda i: (0, 0))],
        out_specs=row(D_MODEL), out_shape=jax.ShapeDtypeStruct((n, D_MODEL), F32),
        compiler_params=_params(1), name="outproj")(x2d, oa, ob, oc, gates, gm, woa, wob, woc, wout, fg2)


def _kmean_kernel(kv_ref, o_ref, *, nb):
    for i in range(nb):
        blk = kv_ref[i * MOBA_BLOCK:(i + 1) * MOBA_BLOCK, :].astype(F32)
        o_ref[i:i + 1, :] = jnp.mean(blk, axis=0, keepdims=True)


def _kmean(kvx2d):
    nblk = kvx2d.shape[0] // MOBA_BLOCK
    nb = SUBLANES if nblk % SUBLANES == 0 else nblk
    w = KV_A * KVX_AB
    return pl.pallas_call(
        functools.partial(_kmean_kernel, nb=nb), grid=(nblk // nb,),
        in_specs=[pl.BlockSpec((nb * MOBA_BLOCK, w), lambda i: (i, OFF_XA // w))],
        out_specs=pl.BlockSpec((nb, w), lambda i: (i, 0)),
        out_shape=jax.ShapeDtypeStruct((nblk, w), F32), compiler_params=_params(1), name="kmean")(kvx2d)


def _cumsum_kernel(x_ref, o_ref, *, chunk):
    t = x_ref.shape[1]
    u = (_iota((chunk, chunk), 0) <= _iota((chunk, chunk), 1)).astype(F32)

    def body(i, carry):
        st = pl.multiple_of(i * chunk, chunk)
        seg = x_ref[:, pl.ds(st, chunk)]
        cs = jnp.dot(seg, u, preferred_element_type=F32, precision=lax.Precision.HIGHEST) + carry
        o_ref[:, pl.ds(st, chunk)] = cs
        return cs[:, chunk - 1:chunk]

    lax.fori_loop(0, t // chunk, body, jnp.zeros((x_ref.shape[0], 1), F32))


def _cumsum_t(xt):
    b, h, t = xt.shape
    chunk = 256 if t % 256 == 0 else t
    spec = pl.BlockSpec((None, h, t), lambda i: (i, 0, 0))
    return pl.pallas_call(functools.partial(_cumsum_kernel, chunk=chunk), grid=(b,), in_specs=[spec], out_specs=spec,
                          out_shape=jax.ShapeDtypeStruct(xt.shape, F32), compiler_params=_params(1),
                          name="cumsum")(xt)


def _tri_tables(nq, kpq):
    qs, ks = [], []
    for qi in range(nq):
        for kj in range((qi + 1) * kpq):
            qs.append(qi)
            ks.append(kj)
    return jnp.asarray(np.asarray(qs, np.int32)), jnp.asarray(np.asarray(ks, np.int32))


def _flash_update(s, vext, m_ref, acc_ref, idx, r0=0):
    rows = slice(r0, r0 + s.shape[0])
    m_prev = m_ref[idx, rows]
    m_next = jnp.maximum(m_prev, jnp.max(s, axis=1, keepdims=True))
    alpha = jnp.exp2(m_prev - m_next)
    p = jnp.exp2(s - jnp.tile(m_next, (1, s.shape[1] // LANES)))
    pv = jnp.dot(p.astype(BF16), vext, preferred_element_type=F32)
    acc_ref[idx, rows] = jnp.tile(alpha, (1, pv.shape[1] // LANES)) * acc_ref[idx, rows] + pv
    m_ref[idx, rows] = m_next


def _init_flash(m_ref, acc_ref):
    m_ref[...] = jnp.full(m_ref.shape, NEG, F32)
    acc_ref[...] = jnp.zeros(acc_ref.shape, F32)


def _tile_update(logits, vext, m_ref, acc_ref, idx, tq, tk, diag):
    if diag is None:
        _flash_update(logits(0, tq, 0, tk), vext, m_ref, acc_ref, idx)
        return
    h = tk // 2
    for r0, c0, c1 in ((diag, 0, h), (diag + h, h, tk)):
        s = logits(r0, tq, c0, c1)
        s = jnp.where(_iota(s.shape, 1) <= _iota(s.shape, 0), s, NEG)
        _flash_update(s, vext[c0:c1], m_ref, acc_ref, idx, r0=r0)


def _when_tile(qi, kj, kpq, update, finalize):
    @pl.when(kj < qi * kpq)
    def _():
        update(None)

    for d in range(kpq):
        @pl.when(kj == qi * kpq + d)
        def _():
            update(d)
            if d == kpq - 1:
                finalize()


def _pad_heads(q_ref, qs_ref, n):
    zeros = jnp.zeros((q_ref.shape[0], HEAD_DIM), BF16)
    for i in range(n):
        qs_ref[i] = jnp.concatenate([q_ref[:, i * HEAD_DIM:(i + 1) * HEAD_DIM], zeros], axis=1)


def _merge_heads(acc_ref):
    outs = []
    for g in range(GROUP):
        a = acc_ref[g]
        outs.append(a / pltpu.roll(a, HEAD_DIM, axis=1))
    low = _iota((1, LANES), 1) < HEAD_DIM
    return jnp.where(low, outs[0], pltpu.roll(outs[1], HEAD_DIM, axis=1))


def _fox_kernel(qi_tab, kj_tab, q_ref, kv_ref, cq_ref, ck_ref, o_ref, qs_ref, m_ref, acc_ref, *, tq, tk):
    step = pl.program_id(2)
    qi, kj = qi_tab[step], kj_tab[step]

    @pl.when(kj == 0)
    def _():
        _init_flash(m_ref, acc_ref)
        _pad_heads(q_ref, qs_ref, GROUP)

    def update(d):
        k = kv_ref[:, :LANES]
        vext = kv_ref[:, LANES:]
        for g in range(GROUP):
            bias = (cq_ref[g][:, 0:1] - ck_ref[g]) * LOG2E
            logits = lambda r0, r1, c0, c1: _dot_t(qs_ref[g, r0:r1], k[c0:c1]) + bias[:, c0:c1]
            _tile_update(logits, vext, m_ref, acc_ref, g, tq, tk, None if d is None else d * tk)

    def finalize():
        o_ref[...] = _merge_heads(acc_ref).astype(o_ref.dtype)

    _when_tile(qi, kj, tq // tk, update, finalize)


def _fox_prompt(q_all, kvx, c_t, *, tq, tk):
    b, t, _ = q_all.shape
    qi_tab, kj_tab = _tri_tables(t // tq, tq // tk)
    xb = OFF_XB // KVX_AB
    grid_spec = pltpu.PrefetchScalarGridSpec(
        num_scalar_prefetch=2, grid=(b, KV_B, int(qi_tab.shape[0])),
        in_specs=[
            pl.BlockSpec((None, tq, LANES), lambda bi, h, s, qt, kt: (bi, qt[s], KV_A + h)),
            pl.BlockSpec((None, tk, KVX_AB), lambda bi, h, s, qt, kt: (bi, kt[s], xb + h)),
            pl.BlockSpec((None, GROUP, 1, tq), lambda bi, h, s, qt, kt: (bi, h, 0, qt[s])),
            pl.BlockSpec((None, GROUP, 1, tk), lambda bi, h, s, qt, kt: (bi, h, 0, kt[s])),
        ],
        out_specs=pl.BlockSpec((None, tq, LANES), lambda bi, h, s, qt, kt: (bi, qt[s], h)),
        scratch_shapes=[pltpu.VMEM((GROUP, tq, LANES), BF16), pltpu.VMEM((GROUP, tq, LANES), F32),
                        pltpu.VMEM((GROUP, tq, LANES), F32)])
    return pl.pallas_call(
        functools.partial(_fox_kernel, tq=tq, tk=tk), grid_spec=grid_spec,
        out_shape=jax.ShapeDtypeStruct((b, t, W_BRANCH), BF16), compiler_params=_params(3),
        name="fox_prompt")(qi_tab, kj_tab, q_all, kvx, c_t[:, :, None, :], c_t[:, :, None, :])


def _diff_kernel(qi_tab, kj_tab, slope_ref, q_ref, kv_ref, lam_ref, sg_ref, o_ref, qs_ref, m_ref, acc_ref, *, tq, tk,
                 lam_init):
    hk = pl.program_id(1)
    step = pl.program_id(2)
    qi, kj = qi_tab[step], kj_tab[step]

    @pl.when(kj == 0)
    def _():
        _init_flash(m_ref, acc_ref)
        _pad_heads(q_ref, qs_ref, 2 * GROUP)

    def update(d):
        vext = kv_ref[:, 2 * LANES:]
        rel = (kj * tk - qi * tq + _iota((1, tk), 1)).astype(F32)
        for g in range(GROUP):
            bias = slope_ref[hk * GROUP + g] * rel
            for m in range(2):
                i = m * GROUP + g
                logits = lambda r0, r1, c0, c1: (
                    _dot_t(qs_ref[i, r0:r1], kv_ref[c0:c1, m * LANES:(m + 1) * LANES]) + bias[:, c0:c1])
                _tile_update(logits, vext, m_ref, acc_ref, i, tq, tk, None if d is None else d * tk)

    def finalize():
        lam = lam_ref[...]
        lam_val = (jnp.exp(jnp.sum(lam[0:1] * lam[1:2], axis=1, keepdims=True))
                   - jnp.exp(jnp.sum(lam[2:3] * lam[3:4], axis=1, keepdims=True)) + lam_init)
        for g in range(GROUP):
            a0, a1 = acc_ref[g], acc_ref[GROUP + g]
            o = a0[:, :V_DIM_C] / a0[:, V_DIM_C:] - lam_val * (a1[:, :V_DIM_C] / a1[:, V_DIM_C:])
            ms = jnp.mean(o * o, axis=-1, keepdims=True)
            o = (o * lax.rsqrt(ms + RMS_EPS)) * sg_ref[...] * (1.0 - lam_init)
            o_ref[:, g * V_DIM_C:(g + 1) * V_DIM_C] = o.astype(o_ref.dtype)

    _when_tile(qi, kj, tq // tk, update, finalize)


def _diff_prompt(q_all, kvx, slopes, lam_l, sg_l, *, tq, tk, lam_init):
    b, t, _ = q_all.shape
    qi_tab, kj_tab = _tri_tables(t // tq, tq // tk)
    xc = OFF_XC // KVX_C
    grid_spec = pltpu.PrefetchScalarGridSpec(
        num_scalar_prefetch=2, grid=(b, KV_C, int(qi_tab.shape[0])),
        in_specs=[
            pl.BlockSpec(memory_space=pltpu.SMEM),
            pl.BlockSpec((None, tq, 2 * LANES), lambda bi, h, s, qt, kt: (bi, qt[s], 4 + h)),
            pl.BlockSpec((None, tk, KVX_C), lambda bi, h, s, qt, kt: (bi, kt[s], xc + h)),
            pl.BlockSpec((4, HEAD_DIM), lambda bi, h, s, qt, kt: (0, 0)),
            pl.BlockSpec((1, V_DIM_C), lambda bi, h, s, qt, kt: (0, 0)),
        ],
        out_specs=pl.BlockSpec((None, tq, 2 * LANES), lambda bi, h, s, qt, kt: (bi, qt[s], h)),
        scratch_shapes=[pltpu.VMEM((2 * GROUP, tq, LANES), BF16), pltpu.VMEM((2 * GROUP, tq, LANES), F32),
                        pltpu.VMEM((2 * GROUP, tq, 2 * V_DIM_C), F32)])
    return pl.pallas_call(
        functools.partial(_diff_kernel, tq=tq, tk=tk, lam_init=lam_init), grid_spec=grid_spec,
        out_shape=jax.ShapeDtypeStruct((b, t, W_BRANCH), BF16), compiler_params=_params(3),
        name="diff_prompt")(qi_tab, kj_tab, slopes, q_all, kvx, lam_l, sg_l)


def _topk_keep(gate, blk, own, past, nblk, axis):
    gm = jnp.where(past, gate, -jnp.inf)
    cnt = jnp.zeros(gate.shape, jnp.int32)
    for jp in range(nblk):
        gj = gm[jp:jp + 1, :] if axis == 0 else gm[:, jp:jp + 1]
        ahead = (gj > gm) | ((gj == gm) & (jp < blk))
        cnt = cnt + ahead.astype(jnp.int32)
    return (past & (cnt < MOBA_TOPK)) | (blk == own)


def _moba_kernel(qi_tab, kj_tab, slope_ref, q_ref, kv_ref, km_ref, o_ref, qa_ref, m_ref, acc_ref, *, tq, tk, nblk):
    hk = pl.program_id(1)
    step = pl.program_id(2)
    qi, kj = qi_tab[step], kj_tab[step]
    bpt = tk // MOBA_BLOCK
    nbp = -(-nblk // SUBLANES) * SUBLANES
    tile = tq

    @pl.when(kj == 0)
    def _():
        _init_flash(m_ref, acc_ref)
        own = lax.shift_right_logical(qi * tile + _iota((1, tile), 1), _log2(MOBA_BLOCK))
        blk = _iota((nbp, 1), 0)
        zeros = jnp.zeros((tile, HEAD_DIM), BF16)
        for g in range(GROUP):
            qg = q_ref[:, g * HEAD_DIM:(g + 1) * HEAD_DIM]
            gate = _dot_t(km_ref[:nbp, :HEAD_DIM], qg.astype(F32), precision=lax.Precision.HIGHEST)
            keep = _topk_keep(gate, blk, own, blk < own, nblk, 0)
            bias = jnp.where(keep, 0.0, NEG)
            bias = jnp.concatenate([bias, jnp.zeros((LANES - nbp, tile), F32)], axis=0)
            qa_ref[g, :, :LANES] = jnp.concatenate([qg, zeros], axis=1)
            qa_ref[g, :, LANES:] = bias.T.astype(BF16)

    def update(d):
        key_blk = kj * bpt + lax.shift_right_logical(_iota((tk, 1), 0), _log2(MOBA_BLOCK))
        onehot = jnp.where(_iota((1, LANES), 1) == key_blk, 1.0, 0.0).astype(BF16)
        kaug = jnp.concatenate([kv_ref[:, :LANES], onehot], axis=1)
        vext = kv_ref[:, LANES:]
        rel = (kj * tk - qi * tq + _iota((1, tk), 1)).astype(F32)
        for g in range(GROUP):
            bias = slope_ref[hk * GROUP + g] * rel
            logits = lambda r0, r1, c0, c1: _dot_t(qa_ref[g, r0:r1], kaug[c0:c1]) + bias[:, c0:c1]
            _tile_update(logits, vext, m_ref, acc_ref, g, tq, tk, None if d is None else d * tk)

    def finalize():
        o_ref[...] = _merge_heads(acc_ref).astype(o_ref.dtype)

    _when_tile(qi, kj, tq // tk, update, finalize)


def _moba_prompt(q_all, kvx, kmean_pad, slopes, *, tq, tk):
    b, t, _ = q_all.shape
    qi_tab, kj_tab = _tri_tables(t // tq, tq // tk)
    grid_spec = pltpu.PrefetchScalarGridSpec(
        num_scalar_prefetch=2, grid=(b, KV_A, int(qi_tab.shape[0])),
        in_specs=[
            pl.BlockSpec(memory_space=pltpu.SMEM),
            pl.BlockSpec((None, tq, LANES), lambda bi, h, s, qt, kt: (bi, qt[s], h)),
            pl.BlockSpec((None, tk, KVX_AB), lambda bi, h, s, qt, kt: (bi, kt[s], h)),
            pl.BlockSpec((None, LANES, KVX_AB), lambda bi, h, s, qt, kt: (bi, 0, h)),
        ],
        out_specs=pl.BlockSpec((None, tq, LANES), lambda bi, h, s, qt, kt: (bi, qt[s], h)),
        scratch_shapes=[pltpu.VMEM((GROUP, tq, 2 * LANES), BF16), pltpu.VMEM((GROUP, tq, LANES), F32),
                        pltpu.VMEM((GROUP, tq, LANES), F32)])
    return pl.pallas_call(
        functools.partial(_moba_kernel, tq=tq, tk=tk, nblk=t // MOBA_BLOCK), grid_spec=grid_spec,
        out_shape=jax.ShapeDtypeStruct((b, t, W_BRANCH), BF16), compiler_params=_params(3),
        name="moba_prompt")(qi_tab, kj_tab, slopes, q_all, kvx, kmean_pad)


def _rows_of(q_ref, cols):
    return jnp.concatenate([q_ref[:, c:c + HEAD_DIM] for c in cols], axis=0)


def _pad_q(q, front=False):
    zeros = jnp.zeros(q.shape, BF16)
    return jnp.concatenate([zeros, q.astype(BF16)] if front else [q.astype(BF16), zeros], axis=1)


def _pair_logits(q, page_a, page_b):
    low = _iota((1, LANES), 1) < HEAD_DIM
    keys = jnp.where(low, page_a, pltpu.roll(page_b, HEAD_DIM, axis=1)).astype(BF16)
    q2 = jnp.concatenate([q, jnp.concatenate([q[:, HEAD_DIM:], q[:, :HEAD_DIM]], axis=1)], axis=0)
    s = _dot_t(q2, keys)
    return [s[:q.shape[0]], s[q.shape[0]:]]


def _col_per_group(vals, n_tok):
    return jnp.concatenate([jnp.full((n_tok, 1), v, F32) for v in vals], axis=0)


def _rows_per_group(x, n_tok):
    return jnp.concatenate([jnp.broadcast_to(x[i:i + 1], (n_tok, x.shape[1])) for i in range(x.shape[0])], axis=0)


def _head_match(rows, cols, nkv, n_tok, col_head):
    row_head = lax.shift_right_logical(_iota((rows, 1), 0), _log2(GROUP * n_tok)) & (nkv - 1)
    return jnp.where(col_head == row_head, 0.0, NEG)


def _new_rows(kvn_ref, nkv):
    w = kvn_ref.shape[1] // nkv
    return jnp.concatenate([kvn_ref[:, h * w:(h + 1) * w] for h in range(nkv)], axis=0)


def _new_bias(rows, nkv, n_tok):
    col = _iota((1, nkv * n_tok), 1)
    bias = _head_match(rows, nkv * n_tok, nkv, n_tok, lax.shift_right_logical(col, _log2(n_tok)))
    visible = (col & (n_tok - 1)) <= (_iota((rows, 1), 0) & (n_tok - 1))
    return jnp.where(visible, bias, NEG), (col & (n_tok - 1)).astype(F32)


def _multi_update(s_list, v_list, m_ref, l_ref, acc_ref, shift=0):
    idx = ...
    m_prev = m_ref[idx]
    smax = s_list[0]
    for s in s_list[1:]:
        smax = jnp.maximum(smax, s)
    m_next = jnp.maximum(m_prev, jnp.max(smax, axis=1, keepdims=True))
    alpha = jnp.exp2(m_prev - m_next)
    psum, pv = None, None
    for s, v in zip(s_list, v_list):
        p = jnp.exp2(s - m_next)
        pm = pltpu.roll(p, shift, axis=1) if shift else p
        d = jnp.dot(pm.astype(BF16), v, preferred_element_type=F32)
        psum = p if psum is None else psum + p
        pv = d if pv is None else pv + d
    l_ref[idx] = alpha * l_ref[idx] + jnp.sum(psum, axis=1, keepdims=True)
    acc_ref[idx] = alpha * acc_ref[idx] + pv
    m_ref[idx] = m_next


def _init_multi(m_ref, l_ref, acc_ref):
    m_ref[...] = jnp.full(m_ref.shape, NEG, F32)
    l_ref[...] = jnp.zeros(l_ref.shape, F32)
    acc_ref[...] = jnp.zeros(acc_ref.shape, F32)


def _fox_sample_kernel(pt_ref, q_ref, kvn_ref, lfn_ref, *rest, npg, n_tok):
    pages, lfs = rest[:npg], rest[npg:2 * npg]
    o_ref, m_ref, l_ref, acc_ref, carry_ref = rest[2 * npg:]
    j = pl.program_id(1)
    rows, pr = H_B * n_tok, PAGE_SIZE * KV_B

    @pl.when(j == 0)
    def _():
        _init_multi(m_ref, l_ref, acc_ref)
        carry_ref[...] = jnp.zeros(carry_ref.shape, F32)

    q = _pad_q(_rows_of(q_ref, [h * HEAD_DIM for h in range(H_B)]))
    col = _iota((1, pr), 1)
    match = _head_match(rows, pr, KV_B, n_tok, col & (KV_B - 1))
    upper = (_iota((PAGE_SIZE, 1), 0) <= lax.shift_right_logical(col, _log2(KV_B))).astype(F32)
    carry = carry_ref[...]
    s_list, v_list = [], []
    c_local = jnp.dot(jnp.concatenate([lf[...] for lf in lfs], axis=0), upper, preferred_element_type=F32,
                      precision=lax.Precision.HIGHEST)
    for r in range(npg):
        c_full = c_local[r * H_B:(r + 1) * H_B] + carry
        carry = c_full[:, pr - 1:pr]
        bias = match - _rows_per_group(c_full, n_tok) * LOG2E
        v_list.append(pages[r][...].astype(BF16))
        s_list.append(_dot_t(q, v_list[-1]) + bias)
    carry_ref[...] = carry
    _multi_update(s_list, v_list, m_ref, l_ref, acc_ref)

    @pl.when(j == pl.num_programs(1) - 1)
    def _():
        kvn = _new_rows(kvn_ref, KV_B).astype(BF16)
        bias, _ = _new_bias(rows, KV_B, n_tok)
        tok = _iota((1, KV_B * n_tok), 1) & (n_tok - 1)
        upper_n = (_iota((n_tok, 1), 0) <= tok).astype(F32)
        c_new = _dot_tl(lfn_ref[...], upper_n) + carry
        s = _dot_t(q, kvn) + (bias - _rows_per_group(c_new, n_tok) * LOG2E)
        _multi_update([s], [kvn], m_ref, l_ref, acc_ref)
        o = acc_ref[:, HEAD_DIM:] / l_ref[...]
        for h in range(H_B):
            o_ref[:, h * HEAD_DIM:(h + 1) * HEAD_DIM] = o[h * n_tok:(h + 1) * n_tok]


def _diff_sample_kernel(pt_ref, slope_ref, lam_ref, sg_ref, q_ref, kvn_ref, *rest, npg, n_tok, lam_init):
    pages = rest[:npg]
    o_ref, m_ref, l_ref, acc_ref = rest[npg:]
    j = pl.program_id(1)
    half = H_C * n_tok
    rows, pr = 2 * half, 2 * PAGE_SIZE * KV_C
    n_past = pl.num_programs(1) * npg * PAGE_SIZE

    @pl.when(j == 0)
    def _():
        _init_multi(m_ref, l_ref, acc_ref)

    qs = [_rows_of(q_ref, [((hk * 2 + m) * GROUP + g) * HEAD_DIM for hk in range(KV_C) for g in range(GROUP)])
          for m in range(2)]
    qpad = jnp.concatenate([_pad_q(qs[0]), _pad_q(qs[1], front=True)], axis=0)
    slope = _col_per_group([slope_ref[h] for h in range(H_C)] * 2, n_tok)

    def logits(kv, bias):
        return jnp.concatenate([_dot_t(qs[m].astype(BF16), kv[:, m * HEAD_DIM:(m + 1) * HEAD_DIM].astype(BF16))
                                for m in range(2)], axis=0) + bias

    col = _iota((1, pr), 1)
    key_head = jnp.where((col & KV_C) == 0, col & (KV_C - 1), -1)
    match = _head_match(rows, pr, KV_C, n_tok, key_head)
    tok = lax.shift_right_logical(col, _log2(2 * KV_C))
    s_list, v_list = [], []
    for r in range(npg):
        rel = ((j * npg + r) * PAGE_SIZE - n_past + tok).astype(F32)
        v_list.append(pages[r][...].astype(BF16))
        s_list.append(_dot_t(qpad, v_list[-1]) + (match + slope * rel))
    _multi_update(s_list, v_list, m_ref, l_ref, acc_ref, shift=KV_C)

    @pl.when(j == pl.num_programs(1) - 1)
    def _():
        kvn = _new_rows(kvn_ref, KV_C)
        bias, rel = _new_bias(rows, KV_C, n_tok)
        _multi_update([logits(kvn, bias + slope * rel)], [kvn[:, 2 * HEAD_DIM:].astype(BF16)], m_ref, l_ref, acc_ref)
        lam = lam_ref[...]
        lam_val = (jnp.exp(jnp.sum(lam[0:1] * lam[1:2], axis=1, keepdims=True))
                   - jnp.exp(jnp.sum(lam[2:3] * lam[3:4], axis=1, keepdims=True)) + lam_init)
        on = acc_ref[...] / l_ref[...]
        o = on[:half] - lam_val * on[half:]
        ms = jnp.mean(o * o, axis=-1, keepdims=True)
        o = (o * lax.rsqrt(ms + RMS_EPS)) * sg_ref[...] * (1.0 - lam_init)
        for h in range(H_C):
            o_ref[:, h * V_DIM_C:(h + 1) * V_DIM_C] = o[h * n_tok:(h + 1) * n_tok]


def _moba_sample_kernel(pt_ref, slope_ref, q_ref, kvn_ref, *rest, npg, n_tok, nblk):
    pages = rest[:npg]
    o_ref, mb_ref, lb_ref, gb_ref, ob_ref = rest[npg:]
    j = pl.program_id(1)
    rows, pr = H_A * n_tok, PAGE_SIZE * KV_A
    ppb = MOBA_BLOCK // PAGE_SIZE
    bps = npg // ppb
    lane = _iota((1, LANES), 1)
    n_past = pl.num_programs(1) * npg * PAGE_SIZE

    @pl.when(j == 0)
    def _():
        mb_ref[...] = jnp.full(mb_ref.shape, NEG, F32)
        lb_ref[...] = jnp.zeros(lb_ref.shape, F32)
        gb_ref[...] = jnp.zeros(gb_ref.shape, F32)

    qf = _rows_of(q_ref, [h * HEAD_DIM for h in range(H_A)])
    q = _pad_q(qf)
    slope = _col_per_group([slope_ref[h] for h in range(H_A)], n_tok)
    col = _iota((1, pr), 1)
    match = _head_match(rows, pr, KV_A, n_tok, col & (KV_A - 1))
    tok = lax.shift_right_logical(col, _log2(KV_A))
    row_kv = lax.shift_right_logical(_iota((rows, 1), 0), _log2(GROUP * n_tok))

    for bl in range(bps):
        blk = j * bps + bl
        s_l, v_l, kmean = [], [], None
        raw = _pair_logits(q, pages[bl * ppb][...], pages[bl * ppb + 1][...])
        for r in range(ppb):
            page = pages[bl * ppb + r]
            rel = ((blk * ppb + r) * PAGE_SIZE - n_past + tok).astype(F32)
            v_l.append(page[...].astype(BF16))
            s_l.append(raw[r] + (match + slope * rel))
            km = jnp.sum(page[:, :HEAD_DIM].reshape(pr // SUBLANES, SUBLANES, HEAD_DIM), axis=0)
            kmean = km if kmean is None else kmean + km
        mb = jnp.max(functools.reduce(jnp.maximum, s_l), axis=1, keepdims=True)
        ps = [jnp.exp2(s - mb) for s in s_l]
        lb = jnp.sum(functools.reduce(jnp.add, ps), axis=1, keepdims=True)
        ob_ref[blk] = functools.reduce(
            jnp.add, [jnp.dot(p.astype(BF16), v, preferred_element_type=F32) for p, v in zip(ps, v_l)])
        kmean = (kmean[:KV_A] + kmean[KV_A:]) * (1.0 / MOBA_BLOCK)
        kmean = jnp.concatenate([kmean, jnp.zeros((SUBLANES - KV_A, HEAD_DIM), F32)], axis=0)
        gate_all = _dot_t(qf, kmean, precision=lax.Precision.HIGHEST)
        gate = jnp.sum(jnp.where(_iota((1, SUBLANES), 1) == row_kv, gate_all, 0.0), axis=1, keepdims=True)
        here = lane == blk
        mb_ref[...] = jnp.where(here, mb, mb_ref[...])
        lb_ref[...] = jnp.where(here, lb, lb_ref[...])
        gb_ref[...] = jnp.where(here, gate, gb_ref[...])

    @pl.when(j == pl.num_programs(1) - 1)
    def _():
        keep = _topk_keep(gb_ref[...], lane, nblk, lane < nblk, nblk, 1) & (lane < nblk)
        m_sel = jnp.where(keep, mb_ref[...], NEG)
        kvn = _new_rows(kvn_ref, KV_A).astype(BF16)
        bias, rel = _new_bias(rows, KV_A, n_tok)
        s = _dot_t(q, kvn) + (bias + slope * rel)
        m_tot = jnp.maximum(jnp.max(s, axis=1, keepdims=True), jnp.max(m_sel, axis=1, keepdims=True))
        p = jnp.exp2(s - m_tot)
        w = jnp.where(keep, jnp.exp2(m_sel - m_tot), 0.0)
        den = jnp.sum(p, axis=1, keepdims=True) + jnp.sum(w * lb_ref[...], axis=1, keepdims=True)
        num = jnp.dot(p.astype(BF16), kvn, preferred_element_type=F32)
        for b in range(nblk):
            num = num + w[:, b:b + 1] * ob_ref[b]
        o = num[:, HEAD_DIM:] / den
        for h in range(H_A):
            o_ref[:, h * HEAD_DIM:(h + 1) * HEAD_DIM] = o[h * n_tok:(h + 1) * n_tok]


def _page_specs(layer, n_pages, npg, page_shape):
    zeros = (0,) * len(page_shape)

    def spec(r):
        return pl.BlockSpec((None, None) + tuple(page_shape),
                            lambda b, j, pt: (layer, pt[b * n_pages + j * npg + r]) + zeros)
    return [spec(r) for r in range(npg)]


def _sample_attention(kind, layer, pt_flat, q_s, kvn, cache, *, n_pages, extra=(), extra_specs=(), logf_cache=None,
                      lfn=None, lam_init=0.0):
    db, n_tok, _ = q_s.shape
    npg = PAGES_PER_STEP if n_pages % PAGES_PER_STEP == 0 else MOBA_BLOCK // PAGE_SIZE
    per_b = lambda w: pl.BlockSpec((None, n_tok, w), lambda b, j, pt: (b, 0, 0))
    qcol = {"moba": 0, "fox": 1, "diff": 2}[kind]
    q_spec = pl.BlockSpec((None, n_tok, W_BRANCH), lambda b, j, pt: (b, 0, qcol))
    in_specs = list(extra_specs) + [q_spec, per_b(W_BRANCH)]
    args = list(extra) + [q_s, kvn]
    if kind == "fox":
        in_specs.append(per_b(H_B))
        args.append(lfn)
    in_specs += _page_specs(layer, n_pages, npg, cache.shape[2:])
    args += [cache] * npg
    col = lambda r, w: pltpu.VMEM((r, w), F32)
    if kind == "fox":
        rows = H_B * n_tok
        in_specs += _page_specs(layer, n_pages, npg, (H_B, PAGE_SIZE))
        args += [logf_cache] * npg
        kern = functools.partial(_fox_sample_kernel, npg=npg, n_tok=n_tok)
        scratch = [col(rows, 1), col(rows, 1), col(rows, 2 * HEAD_DIM), col(H_B, 1)]
    elif kind == "diff":
        rows = 2 * H_C * n_tok
        kern = functools.partial(_diff_sample_kernel, npg=npg, n_tok=n_tok, lam_init=lam_init)
        scratch = [col(rows, 1), col(rows, 1), col(rows, V_DIM_C)]
    else:
        rows = H_A * n_tok
        nblk = n_pages * PAGE_SIZE // MOBA_BLOCK
        kern = functools.partial(_moba_sample_kernel, npg=npg, n_tok=n_tok, nblk=nblk)
        scratch = [col(rows, LANES), col(rows, LANES), col(rows, LANES), pltpu.VMEM((nblk, rows, 2 * HEAD_DIM), F32)]
    grid_spec = pltpu.PrefetchScalarGridSpec(
        num_scalar_prefetch=1, grid=(db, n_pages // npg), in_specs=in_specs, out_specs=per_b(W_BRANCH),
        scratch_shapes=scratch)
    return pl.pallas_call(kern, grid_spec=grid_spec, out_shape=jax.ShapeDtypeStruct((db, n_tok, W_BRANCH), F32),
                          compiler_params=_params(2), name=kind + "_sample")(pt_flat, *args)


def _alibi_slopes(n):
    return 2.0 ** (-8.0 * jnp.arange(1, n + 1, dtype=F32) / n)


def kernel(x_prompt, x_sample, cache_a_kv, cache_b_kv, cache_b_logf, cache_c_kv, page_table, w_in, w_o_a, w_o_b,
           w_o_c, w_out, norm_g, forget_b, diff_lambda, diff_subln_g, final_norm_g):
    depth = w_in.shape[0]
    bp, t, _ = x_prompt.shape
    db, n_tok, _ = x_sample.shape
    n_pages = page_table.shape[1]
    n_pool = cache_a_kv.shape[1]
    assert t % MOBA_BLOCK == 0 and (n_pages * PAGE_SIZE) % MOBA_BLOCK == 0
    assert t // MOBA_BLOCK <= LANES and n_pages * PAGE_SIZE // MOBA_BLOCK <= LANES
    assert n_tok == SUBLANES

    w_perm = _permute_columns(w_in)
    woa, wob, woc, wout = (w.astype(BF16) for w in (w_o_a, w_o_b, w_o_c, w_out))
    norm_g3 = norm_g.reshape(depth, 1, D_MODEL)
    fb3 = jnp.pad(forget_b, ((0, 0), (0, FB_PAD - H_B))).reshape(depth, 1, FB_PAD)
    fg2 = final_norm_g.reshape(1, D_MODEL)
    slopes_a, slopes_c = _alibi_slopes(H_A) * LOG2E, _alibi_slopes(H_C) * LOG2E
    pt_flat = page_table.reshape(-1).astype(jnp.int32)
    ca = cache_a_kv.reshape(depth, n_pool, PAGE_SIZE * KV_A, 2 * HEAD_DIM)
    cb = cache_b_kv.reshape(depth, n_pool, PAGE_SIZE * KV_B, 2 * HEAD_DIM)
    logf_t = jnp.swapaxes(cache_b_logf, 2, 3)
    cc = cache_c_kv.reshape(depth, n_pool, PAGE_SIZE, KV_C, 2, LANES).transpose(0, 1, 2, 4, 3, 5)
    cc = cc.reshape(depth, n_pool, 2 * PAGE_SIZE * KV_C, LANES)

    tm_p = 256 if (bp * t) % 256 == 0 else bp * t
    tk = next((c for c in (1024, 512) if t % c == 0), t)
    tq = Q_TILE_MULT * tk if t % (Q_TILE_MULT * tk) == 0 else tk
    ns = db * n_tok
    smem = pl.BlockSpec(memory_space=pltpu.SMEM)

    xp = x_prompt.reshape(bp * t, D_MODEL)
    xs = x_sample.reshape(ns, D_MODEL)
    outs_p = [[], [], [], []]
    outs_s = [[], [], [], []]
    for l in range(depth):
        lam_init = 0.8 - 0.6 * math.exp(-0.3 * l)
        sg = diff_subln_g[l].reshape(1, V_DIM_C)
        final = l == depth - 1

        kva, kvb, kvc, logf, kvx, q_all, gates, gm = _inproj(xp, l, norm_g3, fb3, w_perm, tm=tm_p, q_dtype=BF16,
                                                             split_heads=True)
        for acc, v in zip(outs_p, (kva, kvb, logf, kvc)):
            acc.append(v)
        q3 = q_all.reshape(bp, t, 3 * W_BRANCH)
        kv3 = kvx.reshape(bp, t, W_KVX)
        km = _kmean(kvx).reshape(bp, t // MOBA_BLOCK, KV_A * KVX_AB)
        km = jnp.pad(km, ((0, 0), (0, LANES - t // MOBA_BLOCK), (0, 0)))
        c_t = _cumsum_t(jnp.swapaxes(logf.reshape(bp, t, H_B), 1, 2))
        oa = _moba_prompt(q3, kv3, km, slopes_a, tq=tq, tk=tk)
        ob = _fox_prompt(q3, kv3, c_t, tq=tq, tk=tk)
        oc = _diff_prompt(q3, kv3, slopes_c, diff_lambda[l], sg, tq=tk, tk=tk, lam_init=lam_init)
        xp = _outproj(xp, oa.reshape(bp * t, W_BRANCH), ob.reshape(bp * t, W_BRANCH), oc.reshape(bp * t, W_BRANCH),
                      gates, gm, l, woa, wob, woc, wout, fg2, tm=tm_p, final=final)

        kva, kvb, kvc, logf, _, q_all, gates, gm = _inproj(xs, l, norm_g3, fb3, w_perm, tm=ns, q_dtype=F32,
                                                           split_heads=False)
        for acc, v in zip(outs_s, (kva, kvb, logf, kvc)):
            acc.append(v)
        q3 = q_all.reshape(db, n_tok, 3 * W_BRANCH)
        r3 = lambda a: a.reshape(db, n_tok, a.shape[-1])
        oa = _sample_attention("moba", l, pt_flat, q3, r3(kva), ca, n_pages=n_pages, extra=(slopes_a,),
                               extra_specs=(smem,))
        ob = _sample_attention("fox", l, pt_flat, q3, r3(kvb), cb, n_pages=n_pages, logf_cache=logf_t,
                               lfn=r3(logf))
        oc = _sample_attention("diff", l, pt_flat, q3, r3(kvc), cc, n_pages=n_pages,
                               extra=(slopes_c, diff_lambda[l], sg),
                               extra_specs=(smem, pl.BlockSpec((4, HEAD_DIM), lambda b, j, pt: (0, 0)),
                                            pl.BlockSpec((1, V_DIM_C), lambda b, j, pt: (0, 0))),
                               lam_init=lam_init)
        xs = _outproj(xs, oa.reshape(ns, W_BRANCH), ob.reshape(ns, W_BRANCH), oc.reshape(ns, W_BRANCH), gates, gm, l,
                      woa, wob, woc, wout, fg2, tm=ns, final=final)

    def stack(vals, b_, t_, tail):
        return jnp.stack(vals).reshape((depth, b_, t_) + tail)

    pa, pb, pf, pc = outs_p
    sa, sb, sf, sc = outs_s
    kv_tail, c_tail = (KV_A, 2 * HEAD_DIM), (KV_C, 2 * HEAD_DIM + V_DIM_C)
    return (xp.reshape(bp, t, D_MODEL), xs.reshape(db, n_tok, D_MODEL),
            stack(pa, bp, t, kv_tail), stack(pb, bp, t, kv_tail), stack(pf, bp, t, (H_B,)), stack(pc, bp, t, c_tail),
            stack(sa, db, n_tok, kv_tail), stack(sb, db, n_tok, kv_tail), stack(sf, db, n_tok, (H_B,)),
            stack(sc, db, n_tok, c_tail))
```
